```python
import math
import jax, jax.numpy as jnp
from jax import lax
import numpy as np

D_MODEL = 1024
BATCH = 8
SEQ = 8192
DEPTH = 2

CTX_LEN = 256
GRID_W = 64
HEAD_DIM = 64
A_Q_HEADS = 6
A_KV_HEADS = 2
F_GROUPS = 4
F_GROUP_CH = 64
C_Q_HEADS = 6
C_KV_HEADS = 2
WINDOW = 128
Q_BLOCK = 128
ROPE_THETA = 10000.0
AXIS_ROT = HEAD_DIM // 2
D_FF = -(-8 * D_MODEL // (3 * 256)) * 256
NORM_EPS = 1e-6
NEG_INF = -1e30

Q_A_W = A_Q_HEADS * HEAD_DIM
KV_A_W = A_KV_HEADS * HEAD_DIM
F_W = F_GROUPS * F_GROUP_CH
Q_C_W = C_Q_HEADS * HEAD_DIM
KV_C_W = C_KV_HEADS * HEAD_DIM
IN_WIDTH = Q_A_W + 2 * KV_A_W + F_W + Q_C_W + 2 * KV_C_W
MIX_WIDTH = Q_A_W + F_W + Q_C_W

kernel_name = 'hybrid_dit_global_fourier_window_heads'


def rms_norm(x, g):
    xf = x.astype(jnp.float32)
    y = xf * lax.rsqrt(jnp.mean(xf * xf, axis=-1, keepdims=True) + NORM_EPS)
    return (y * g.astype(jnp.float32)).astype(x.dtype)


def adaln(cond, w_ada, b_ada):
    m = jax.nn.silu(cond) @ w_ada + b_ada
    return jnp.split(m, 6, axis=-1)


def axial_rope_tables(n_tokens):
    rows = n_tokens // GRID_W
    row = jnp.repeat(jnp.arange(rows, dtype=jnp.float32), GRID_W)
    col = jnp.tile(jnp.arange(GRID_W, dtype=jnp.float32), rows)
    inv_freq = ROPE_THETA ** (-jnp.arange(0, AXIS_ROT, 2, dtype=jnp.float32) / AXIS_ROT)
    ang = jnp.concatenate([row[:, None] * inv_freq, col[:, None] * inv_freq], axis=-1)
    return jnp.cos(ang), jnp.sin(ang)


def apply_rope(x, cos, sin):
    xf = x.astype(jnp.float32)
    half = HEAD_DIM // 2
    x1, x2 = xf[..., :half], xf[..., half:]
    c = cos[None, :, None, :]
    s = sin[None, :, None, :]
    return jnp.concatenate([x1 * c - x2 * s, x2 * c + x1 * s], axis=-1).astype(x.dtype)


def gqa_logits(q, k):
    b, lq, h, dh = q.shape
    kvh = k.shape[2]
    qg = q.reshape(b, lq, kvh, h // kvh, dh)
    return jnp.einsum('bqhgd,bkhd->bhgqk', qg, k, preferred_element_type=jnp.float32) * (dh ** -0.5)


def gqa_combine(p, v):
    o = jnp.einsum('bhgqk,bkhd->bqhgd', p.astype(v.dtype), v)
    return o.reshape(o.shape[0], o.shape[1], -1)


def sink_logits(sink, like):
    kvh, g = like.shape[1], like.shape[2]
    return jnp.broadcast_to(sink.astype(jnp.float32).reshape(1, kvh, g, 1, 1), like.shape[:-1] + (1,))


def dense_attention(q, k, v, sink=None):
    s = gqa_logits(q, k)
    if sink is None:
        return gqa_combine(jax.nn.softmax(s, axis=-1), v)
    s = jnp.concatenate([s, sink_logits(sink, s)], axis=-1)
    p = jax.nn.softmax(s, axis=-1)[..., :-1]
    return gqa_combine(p, v)


def global_attention(q, k, v, k_ctx, v_ctx):
    b, s, h, dh = q.shape
    nb = s // Q_BLOCK
    k_all = jnp.concatenate([k_ctx, k], axis=1)
    v_all = jnp.concatenate([v_ctx, v], axis=1)
    qb = jnp.moveaxis(q.reshape(b, nb, Q_BLOCK, h, dh), 1, 0)
    ob = lax.map(lambda qi: dense_attention(qi, k_all, v_all), qb)
    return jnp.moveaxis(ob, 0, 1).reshape(b, s, h * dh)


def window_attention(q, k, v, k_ctx, v_ctx, sink):
    b, s, h, dh = q.shape
    nb = s // Q_BLOCK
    band = Q_BLOCK + 2 * WINDOW
    n_ctx = k_ctx.shape[1]
    pad = ((0, 0), (WINDOW, WINDOW), (0, 0), (0, 0))
    kp = jnp.pad(k, pad)
    vp = jnp.pad(v, pad)
    qb = jnp.moveaxis(q.reshape(b, nb, Q_BLOCK, h, dh), 1, 0)
    qi_idx = jnp.arange(Q_BLOCK)[:, None]
    kj = jnp.arange(band)[None, :]
    in_window = jnp.abs(kj - WINDOW - qi_idx) <= WINDOW

    def one_block(args):
        qi, n = args
        start = n * Q_BLOCK
        kb = lax.dynamic_slice_in_dim(kp, start, band, axis=1)
        vb = lax.dynamic_slice_in_dim(vp, start, band, axis=1)
        kpos = start - WINDOW + kj
        valid = in_window & (kpos >= 0) & (kpos < s)
        s_loc = jnp.where(valid, gqa_logits(qi, kb), NEG_INF)
        s_ctx = gqa_logits(qi, k_ctx)
        logits = jnp.concatenate([s_loc, s_ctx, sink_logits(sink, s_loc)], axis=-1)
        p = jax.nn.softmax(logits, axis=-1)
        return gqa_combine(p[..., :band], vb) + gqa_combine(p[..., band:band + n_ctx], v_ctx)

    ob = lax.map(one_block, (qb, jnp.arange(nb)))
    return jnp.moveaxis(ob, 0, 1).reshape(b, s, h * dh)


def fourier_mix(u):
    b, l, _ = u.shape
    ug = u.astype(jnp.float32).reshape(b, l, F_GROUPS, F_GROUP_CH)
    f = jnp.fft.fftn(ug, axes=(1, 3), norm='ortho').real
    return f.reshape(b, l, F_W).astype(u.dtype)


def mixer_inputs(h, w_in, q_norm, k_norm, rope):
    b, l, _ = h.shape
    widths = (Q_A_W, KV_A_W, KV_A_W, F_W, Q_C_W, KV_C_W, KV_C_W)
    points = [int(p) for p in np.cumsum(widths)[:-1]]
    qa, ka, va, fb, qc, kc, vc = jnp.split(h @ w_in, points, axis=-1)
    qa = rms_norm(qa.reshape(b, l, A_Q_HEADS, HEAD_DIM), q_norm)
    ka = rms_norm(ka.reshape(b, l, A_KV_HEADS, HEAD_DIM), k_norm)
    va = va.reshape(b, l, A_KV_HEADS, HEAD_DIM)
    qc = qc.reshape(b, l, C_Q_HEADS, HEAD_DIM)
    kc = kc.reshape(b, l, C_KV_HEADS, HEAD_DIM)
    vc = vc.reshape(b, l, C_KV_HEADS, HEAD_DIM)
    if rope is not None:
        cos, sin = rope
        qa, ka = apply_rope(qa, cos, sin), apply_rope(ka, cos, sin)
        qc, kc = apply_rope(qc, cos, sin), apply_rope(kc, cos, sin)
    return qa, ka, va, fb, qc, kc, vc


def swiglu(h, w_gate, w_up, w_down):
    return (jax.nn.silu(h @ w_gate) * (h @ w_up)) @ w_down


def setup_inputs(seed: int = 0) -> dict:
    key = jax.random.key(seed)
    ks = jax.random.split(key, 18)
    n = jax.random.normal
    f32 = jnp.float32
    return {
        'x': n(ks[0], (BATCH, SEQ, D_MODEL), f32),
        'c': n(ks[1], (BATCH, D_MODEL), f32),
        'ctx': n(ks[2], (BATCH, CTX_LEN, D_MODEL), f32),
        'c_ctx': n(ks[3], (D_MODEL,), f32),
        'w_ada': 0.02 * n(ks[4], (DEPTH, D_MODEL, 6 * D_MODEL), f32),
        'b_ada': 0.01 * n(ks[5], (DEPTH, 6 * D_MODEL), f32),
        'g_mix': 1.0 + 0.05 * n(ks[6], (DEPTH, D_MODEL), f32),
        'g_ffn': 1.0 + 0.05 * n(ks[7], (DEPTH, D_MODEL), f32),
        'w_in': n(ks[8], (DEPTH, D_MODEL, IN_WIDTH), f32) * D_MODEL ** -0.5,
        'q_norm': 1.0 + 0.05 * n(ks[9], (DEPTH, HEAD_DIM), f32),
        'k_norm': 1.0 + 0.05 * n(ks[10], (DEPTH, HEAD_DIM), f32),
        'sink': 0.5 * n(ks[11], (DEPTH, C_Q_HEADS), f32),
        'w_out': n(ks[12], (DEPTH, MIX_WIDTH, D_MODEL), f32) * MIX_WIDTH ** -0.5,
        'w_gate': n(ks[13], (DEPTH, D_MODEL, D_FF), f32) * D_MODEL ** -0.5,
        'w_up': n(ks[14], (DEPTH, D_MODEL, D_FF), f32) * D_MODEL ** -0.5,
        'w_down': n(ks[15], (DEPTH, D_FF, D_MODEL), f32) * D_FF ** -0.5,
        'g_final': 1.0 + 0.05 * n(ks[16], (D_MODEL,), f32),
    }


def reference(x, c, ctx, c_ctx, w_ada, b_ada, g_mix, g_ffn, w_in, q_norm, k_norm, sink, w_out, w_gate, w_up, w_down, g_final):
    rope = axial_rope_tables(x.shape[1])
    for l in range(DEPTH):
        update_ctx = l < DEPTH - 1
        sh1, sc1, gt1, sh2, sc2, gt2 = [m[:, None, :] for m in adaln(c, w_ada[l], b_ada[l])]
        csh1, csc1, cgt1, csh2, csc2, cgt2 = adaln(c_ctx, w_ada[l], b_ada[l])
        h = rms_norm(x, g_mix[l]) * (1 + sc1) + sh1
        hc = rms_norm(ctx, g_mix[l]) * (1 + csc1) + csh1
        qa, ka, va, fb, qc, kc, vc = mixer_inputs(h, w_in[l], q_norm[l], k_norm[l], rope)
        qac, kac, vac, fbc, qcc, kcc, vcc = mixer_inputs(hc, w_in[l], q_norm[l], k_norm[l], None)
        o = jnp.concatenate([
            global_attention(qa, ka, va, kac, vac),
            fourier_mix(fb),
            window_attention(qc, kc, vc, kcc, vcc, sink[l]),
        ], axis=-1)
        x = x + gt1 * (o @ w_out[l])
        x = x + gt2 * swiglu(rms_norm(x, g_ffn[l]) * (1 + sc2) + sh2, w_gate[l], w_up[l], w_down[l])
        if update_ctx:
            oc = jnp.concatenate([
                dense_attention(qac, kac, vac),
                fourier_mix(fbc),
                dense_attention(qcc, kcc, vcc, sink[l]),
            ], axis=-1)
            ctx = ctx + cgt1 * (oc @ w_out[l])
            ctx = ctx + cgt2 * swiglu(rms_norm(ctx, g_ffn[l]) * (1 + csc2) + csh2, w_gate[l], w_up[l], w_down[l])
    return rms_norm(x, g_final)
```

```python
import functools
import math

import jax
import jax.numpy as jnp
import numpy as np
from jax import lax
from jax.experimental import pallas as pl
from jax.experimental.pallas import tpu as pltpu

F32 = jnp.float32
BF16 = jnp.bfloat16

HEAD_DIM = 64
HALF = HEAD_DIM // 2
Q_HEADS = 6
KV_HEADS = 2
GROUP = Q_HEADS // KV_HEADS
QW = Q_HEADS * HEAD_DIM
KVW = KV_HEADS * HEAD_DIM
F_GROUPS = 4
F_CH = 64
FW = F_GROUPS * F_CH
GRID_W = 64
WINDOW = 128
ROPE_THETA = 10000.0
NORM_EPS = 1e-6
NEG_INF = -1e30
V_ROWS = HEAD_DIM + 16
FFT_L2 = 128
T_WIDTH = 2 * (QW + 2 * KVW)

VMEM_LIMIT = 56 * 1024 * 1024


def _cparams(sem):
    return pltpu.CompilerParams(dimension_semantics=sem, vmem_limit_bytes=VMEM_LIMIT)


def _rms(x, g):
    return x * lax.rsqrt(jnp.mean(x * x, axis=-1, keepdims=True) + NORM_EPS) * g


def _adaln_kernel(cond_ref, w_ref, b_ref, out_ref):
    cnd = cond_ref[...]
    a = cnd / (1.0 + jnp.exp(-cnd))
    out_ref[0] = jnp.dot(a, w_ref[0], preferred_element_type=F32, precision=lax.Precision.HIGHEST) + b_ref[0]


def _adaln(cond, w_ada, b_ada):
    depth, d, n6 = w_ada.shape
    rows = cond.shape[0]
    tn = 768
    return pl.pallas_call(
        _adaln_kernel,
        grid=(depth, n6 // tn),
        in_specs=[
            pl.BlockSpec((rows, d), lambda l, j: (0, 0)),
            pl.BlockSpec((1, d, tn), lambda l, j: (l, 0, j)),
            pl.BlockSpec((1, 1, tn), lambda l, j: (l, 0, j)),
        ],
        out_specs=pl.BlockSpec((1, rows, tn), lambda l, j: (l, 0, j)),
        out_shape=jax.ShapeDtypeStruct((depth, rows, n6), F32),
        compiler_params=_cparams(("arbitrary", "arbitrary")),
        name="adaln",
    )(cond, w_ada, b_ada.reshape(depth, 1, n6))


def _rope_t(x, cos, sin):
    x1, x2 = x[:HALF], x[HALF:]
    return jnp.concatenate([x1 * cos - x2 * sin, x2 * cos + x1 * sin], axis=0)


def _head_norm_t(x, g):
    return x * lax.rsqrt(jnp.mean(x * x, axis=0, keepdims=True) + NORM_EPS) * g


def _mixin_kernel(x_ref, mods_ref, g_ref, wt_ref, wn_ref, qn_ref, kn_ref, cos_ref, sin_ref,
                  qa_ref, ka_ref, va_ref, fb_ref, qc_ref, kc_ref, vc_ref):
    tm = x_ref.shape[1]
    sh1, sc1 = mods_ref[0, 0:1, :], mods_ref[0, 1:2, :]
    h = (_rms(x_ref[0], g_ref[...]) * (1.0 + sc1) + sh1).astype(BF16)
    yt = lax.dot_general(wt_ref[...], h, (((1,), (1,)), ((), ())), preferred_element_type=F32)
    fb_ref[0] = jnp.dot(h, wn_ref[...], preferred_element_type=F32).astype(BF16)
    cos, sin = cos_ref[...], sin_ref[...]
    qn = jnp.broadcast_to(qn_ref[...], (HEAD_DIM, tm))
    kn = jnp.broadcast_to(kn_ref[...], (HEAD_DIM, tm))
    ones = jnp.ones((V_ROWS - HEAD_DIM, tm), BF16)
    scale = HEAD_DIM ** -0.5

    def head(base, i):
        return yt[base + i * HEAD_DIM: base + (i + 1) * HEAD_DIM, :]

    base = 0
    for i in range(Q_HEADS):
        q = _rope_t(_head_norm_t(head(base, i), qn), cos, sin) * scale
        qa_ref[0, i * HEAD_DIM:(i + 1) * HEAD_DIM, :] = q.astype(BF16)
    base += QW
    k = jnp.concatenate([_rope_t(_head_norm_t(head(base, i), kn), cos, sin) for i in range(KV_HEADS)], axis=0)
    ka_ref[0] = k.T.astype(BF16)
    base += KVW
    for i in range(KV_HEADS):
        va_ref[0, i, :HEAD_DIM, :] = head(base, i).astype(BF16)
        va_ref[0, i, HEAD_DIM:, :] = ones
    base += KVW
    for i in range(Q_HEADS):
        q = _rope_t(head(base, i), cos, sin) * scale
        qc_ref[0, i * HEAD_DIM:(i + 1) * HEAD_DIM, :] = q.astype(BF16)
    base += QW
    k = jnp.concatenate([_rope_t(head(base, i), cos, sin) for i in range(KV_HEADS)], axis=0)
    kc_ref[0] = k.T.astype(BF16)
    base += KVW
    for i in range(KV_HEADS):
        vc_ref[0, i, :HEAD_DIM, :] = head(base, i).astype(BF16)
        vc_ref[0, i, HEAD_DIM:, :] = ones


def _mixer_inputs(x, mods, mod_row, g, wt, wn, qn, kn, cos_t, sin_t, tm):
    b, l, d = x.shape
    tm = min(tm, l)
    bs_qt = pl.BlockSpec((1, QW, tm), lambda bi, j: (bi, 0, j))
    bs_k = pl.BlockSpec((1, tm, KVW), lambda bi, j: (bi, j, 0))
    bs_vt = pl.BlockSpec((1, KV_HEADS, V_ROWS, tm), lambda bi, j: (bi, 0, 0, j))
    sd = jax.ShapeDtypeStruct
    return pl.pallas_call(
        _mixin_kernel,
        grid=(b, l // tm),
        in_specs=[
            pl.BlockSpec((1, tm, d), lambda bi, j: (bi, j, 0)),
            pl.BlockSpec((1, 6, d), lambda bi, j: (mod_row(bi), 0, 0)),
            pl.BlockSpec((1, d), lambda bi, j: (0, 0)),
            pl.BlockSpec(wt.shape, lambda bi, j: (0, 0)),
            pl.BlockSpec(wn.shape, lambda bi, j: (0, 0)),
            pl.BlockSpec((HEAD_DIM, 1), lambda bi, j: (0, 0)),
            pl.BlockSpec((HEAD_DIM, 1), lambda bi, j: (0, 0)),
            pl.BlockSpec((HALF, tm), lambda bi, j: (0, j)),
            pl.BlockSpec((HALF, tm), lambda bi, j: (0, j)),
        ],
        out_specs=[bs_qt, bs_k, bs_vt, pl.BlockSpec((1, tm, FW), lambda bi, j: (bi, j, 0)), bs_qt, bs_k, bs_vt],
        out_shape=[sd((b, QW, l), BF16), sd((b, l, KVW), BF16), sd((b, KV_HEADS, V_ROWS, l), BF16),
                   sd((b, l, FW), BF16),
                   sd((b, QW, l), BF16), sd((b, l, KVW), BF16), sd((b, KV_HEADS, V_ROWS, l), BF16)],
        compiler_params=_cparams(("parallel", "arbitrary")),
        name="mixer_inputs",
    )(x, mods, g, wt, wn, qn, kn, cos_t, sin_t)


def _softmax_step(s, vt, m_ref, acc_ref):
    m_old = m_ref[...]
    m_new = jnp.maximum(m_old, jnp.max(s, axis=0, keepdims=True))
    p = jnp.exp(s - m_new).astype(BF16)
    acc_ref[...] = acc_ref[...] * jnp.exp(m_old - m_new) + jnp.dot(vt, p, preferred_element_type=F32)
    m_ref[...] = m_new


def _attn_kernel(*refs, bq, bk, n_main, window, has_sink, seq):
    refs = list(refs)
    qt_ref, kctx_ref, vctx_ref = refs[:3]
    pos = 3
    if n_main or window:
        k_ref, vt_ref = refs[pos:pos + 2]
        pos += 2
    if has_sink:
        sink_ref = refs[pos]
        pos += 1
    o_ref, acc_ref, m_ref, ot_ref = refs[pos:pos + 4]
    n = GROUP * bq
    zeros_pad = jnp.zeros((HEAD_DIM, n), BF16)

    if window:
        band = bq + 2 * WINDOW
        q0 = pl.program_id(1) * bq
        start = pl.multiple_of(jnp.clip(q0 - WINDOW, 0, seq - band), WINDOW)
        kpos = start + lax.broadcasted_iota(jnp.int32, (band, bq), 0)
        qpos = q0 + lax.broadcasted_iota(jnp.int32, (band, bq), 1)
        bias1 = jnp.where(jnp.abs(kpos - qpos) <= WINDOW, 0.0, NEG_INF).astype(F32)
        bias = jnp.concatenate([bias1] * GROUP, axis=1)

    for g in range(KV_HEADS):
        qg = jnp.concatenate([qt_ref[0, (GROUP * g + h) * HEAD_DIM:(GROUP * g + h + 1) * HEAD_DIM, :]
                              for h in range(GROUP)], axis=1)
        qpad = jnp.concatenate([qg, zeros_pad] if g == 0 else [zeros_pad, qg], axis=0)

        if has_sink:
            m_ref[...] = jnp.concatenate(
                [jnp.full((1, bq), sink_ref[GROUP * g + h], F32) for h in range(GROUP)], axis=1)
            acc_ref[...] = jnp.concatenate(
                [jnp.zeros((HEAD_DIM, n), F32), jnp.ones((V_ROWS - HEAD_DIM, n), F32)], axis=0)
        else:
            m_ref[...] = jnp.full((1, n), NEG_INF, F32)
            acc_ref[...] = jnp.zeros((V_ROWS, n), F32)

        _softmax_step(jnp.dot(kctx_ref[0], qpad, preferred_element_type=F32), vctx_ref[0, g], m_ref, acc_ref)

        if window:
            s = jnp.dot(k_ref[0, pl.ds(start, band), :], qpad, preferred_element_type=F32) + bias
            _softmax_step(s, vt_ref[0, g, :, pl.ds(start, band)], m_ref, acc_ref)
        elif n_main:
            def body(j, carry):
                off = pl.multiple_of(j * bk, bk)
                s = jnp.dot(k_ref[0, pl.ds(off, bk), :], qpad, preferred_element_type=F32)
                _softmax_step(s, vt_ref[0, g, :, pl.ds(off, bk)], m_ref, acc_ref)
                return carry
            lax.fori_loop(0, n_main, body, 0)

        acc = acc_ref[...]
        o = acc[:HEAD_DIM] / acc[HEAD_DIM:HEAD_DIM + 1]
        for h in range(GROUP):
            ot_ref[(GROUP * g + h) * HEAD_DIM:(GROUP * g + h + 1) * HEAD_DIM, :] = o[:, h * bq:(h + 1) * bq]

    o_ref[0] = ot_ref[...].T.astype(BF16)


def _attention(qt, k_ctx, vt_ctx, k=None, vt=None, sink=None, *, window=False, bq=256, bk=512, name="attn"):
    b, _, lq = qt.shape
    c = k_ctx.shape[1]
    bq = min(bq, lq)
    n_main = 0
    seq = 0
    args = [qt, k_ctx, vt_ctx]
    in_specs = [
        pl.BlockSpec((1, QW, bq), lambda bi, i: (bi, 0, i)),
        pl.BlockSpec((1, c, KVW), lambda bi, i: (bi, 0, 0)),
        pl.BlockSpec((1, KV_HEADS, V_ROWS, c), lambda bi, i: (bi, 0, 0, 0)),
    ]
    if k is not None:
        seq = k.shape[1]
        bk = min(bk, seq)
        n_main = 0 if window else seq // bk
        args += [k, vt]
        in_specs += [
            pl.BlockSpec((1, seq, KVW), lambda bi, i: (bi, 0, 0)),
            pl.BlockSpec((1, KV_HEADS, V_ROWS, seq), lambda bi, i: (bi, 0, 0, 0)),
        ]
    if sink is not None:
        args.append(sink)
        in_specs.append(pl.BlockSpec(memory_space=pltpu.SMEM))
    n = GROUP * bq
    kern = functools.partial(_attn_kernel, bq=bq, bk=bk, n_main=n_main, window=window,
                             has_sink=sink is not None, seq=seq)
    return pl.pallas_call(
        kern,
        grid=(b, lq // bq),
        in_specs=in_specs,
        out_specs=pl.BlockSpec((1, bq, QW), lambda bi, i: (bi, i, 0)),
        out_shape=jax.ShapeDtypeStruct((b, lq, QW), BF16),
        scratch_shapes=[pltpu.VMEM((V_ROWS, n), F32), pltpu.VMEM((1, n), F32), pltpu.VMEM((QW, bq), F32)],
        compiler_params=_cparams(("parallel", "arbitrary")),
        name=name,
    )(*args)


def _dft_cs(n):
    idx = np.arange(n)
    ang = 2.0 * np.pi * ((idx[:, None] * idx[None, :]) % n) / n
    return np.cos(ang), np.sin(ang)


def _channel_table():
    c, s = _dft_cs(F_CH)
    eye = np.eye(F_GROUPS)
    return np.concatenate([np.kron(eye, c), np.kron(eye, s)], axis=0)


def _fft_stage1_kernel(w_ref, x_ref, y_ref):
    y_ref[0] = jnp.dot(w_ref[...], x_ref[0], preferred_element_type=F32).astype(BF16)


def _fft_stage2_kernel(t_ref, cs_ref, y_ref, o_ref, *, kc, scale):
    for kk in range(kc):
        a = jnp.concatenate([y_ref[0, 0, kk], y_ref[0, 1, kk]], axis=0)
        z = jnp.dot(t_ref[kk], a, preferred_element_type=F32)
        zc = jnp.concatenate([z[:FFT_L2], z[FFT_L2:]], axis=1).astype(BF16)
        o = jnp.dot(zc, cs_ref[...], preferred_element_type=F32) * scale
        o_ref[0, :, kk * FW:(kk + 1) * FW] = o.astype(BF16)


def _fourier_mix(u):
    b, l, _ = u.shape
    l1 = l // FFT_L2
    c1, s1 = _dft_cs(l1)
    w1 = jnp.asarray(np.concatenate([c1, -s1], axis=0), BF16)
    k = (np.arange(l1)[:, None, None] + l1 * np.arange(FFT_L2)[None, :, None])
    ang = 2.0 * np.pi * ((k * np.arange(FFT_L2)[None, None, :]) % l) / l
    mc, ms = np.cos(ang), np.sin(ang)
    t2 = jnp.asarray(np.concatenate([np.concatenate([mc, ms], axis=2),
                                     np.concatenate([-ms, mc], axis=2)], axis=1), BF16)
    cs = jnp.asarray(_channel_table(), BF16)

    cols = FFT_L2 * FW
    tc = min(4096, cols)
    y = pl.pallas_call(
        _fft_stage1_kernel,
        grid=(b, cols // tc),
        in_specs=[pl.BlockSpec((2 * l1, l1), lambda bi, j: (0, 0)),
                  pl.BlockSpec((1, l1, tc), lambda bi, j: (bi, 0, j))],
        out_specs=pl.BlockSpec((1, 2 * l1, tc), lambda bi, j: (bi, 0, j)),
        out_shape=jax.ShapeDtypeStruct((b, 2 * l1, cols), BF16),
        compiler_params=_cparams(("parallel", "arbitrary")),
        name="fft_stage1",
    )(w1, u.reshape(b, l1, cols))

    kc = min(8, l1)
    scale = 1.0 / math.sqrt(l * F_CH)
    out = pl.pallas_call(
        functools.partial(_fft_stage2_kernel, kc=kc, scale=scale),
        grid=(l1 // kc, b),
        in_specs=[pl.BlockSpec((kc, 2 * FFT_L2, 2 * FFT_L2), lambda j, bi: (j, 0, 0)),
                  pl.BlockSpec((2 * FW, FW), lambda j, bi: (0, 0)),
                  pl.BlockSpec((1, 2, kc, FFT_L2, FW), lambda j, bi: (bi, 0, j, 0, 0))],
        out_specs=pl.BlockSpec((1, FFT_L2, kc * FW), lambda j, bi: (bi, 0, j)),
        out_shape=jax.ShapeDtypeStruct((b, FFT_L2, l1 * FW), BF16),
        compiler_params=_cparams(("arbitrary", "arbitrary")),
        name="fft_stage2",
    )(t2, cs, y.reshape(b, 2, l1, FFT_L2, FW))
    return out.reshape(b, l, FW)


def _fourier_small_kernel(wl_ref, cst_ref, u_ref, o_ref, *, scale):
    p = jnp.dot(u_ref[0], cst_ref[...], preferred_element_type=F32)
    pp = jnp.concatenate([p[:, :FW], p[:, FW:]], axis=0).astype(BF16)
    o_ref[0] = (jnp.dot(wl_ref[...], pp, preferred_element_type=F32) * scale).astype(BF16)


def _fourier_mix_small(u):
    b, l, _ = u.shape
    cl, sl = _dft_cs(l)
    wl = jnp.asarray(np.concatenate([cl, -sl], axis=1), BF16)
    cst = jnp.asarray(_channel_table().reshape(2, FW, FW).transpose(1, 0, 2).reshape(FW, 2 * FW), BF16)
    return pl.pallas_call(
        functools.partial(_fourier_small_kernel, scale=1.0 / math.sqrt(l * F_CH)),
        grid=(b,),
        in_specs=[pl.BlockSpec((l, 2 * l), lambda bi: (0, 0)),
                  pl.BlockSpec((FW, 2 * FW), lambda bi: (0, 0)),
                  pl.BlockSpec((1, l, FW), lambda bi: (bi, 0, 0))],
        out_specs=pl.BlockSpec((1, l, FW), lambda bi: (bi, 0, 0)),
        out_shape=jax.ShapeDtypeStruct((b, l, FW), BF16),
        compiler_params=_cparams(("parallel",)),
        name="fourier_ctx",
    )(wl, cst, u)


def _ffn_kernel(x_ref, og_ref, of_ref, ow_ref, mods_ref, g_ref, wo_ref, wg_ref, wu_ref, wd_ref, gf_ref, out_ref,
                *, final_norm):
    gt1, sh2 = mods_ref[0, 2:3, :], mods_ref[0, 3:4, :]
    sc2, gt2 = mods_ref[0, 4:5, :], mods_ref[0, 5:6, :]
    o = jnp.concatenate([og_ref[0], of_ref[0], ow_ref[0]], axis=1)
    x1 = x_ref[0] + gt1 * jnp.dot(o, wo_ref[...], preferred_element_type=F32)
    hn = (_rms(x1, g_ref[...]) * (1.0 + sc2) + sh2).astype(BF16)
    gate = jnp.dot(hn, wg_ref[...], preferred_element_type=F32)
    up = jnp.dot(hn, wu_ref[...], preferred_element_type=F32)
    act = (gate / (1.0 + jnp.exp(-gate)) * up).astype(BF16)
    x2 = x1 + gt2 * jnp.dot(act, wd_ref[...], preferred_element_type=F32)
    if final_norm:
        x2 = _rms(x2, gf_ref[...])
    out_ref[0] = x2


def _out_ffn(x, og, of, ow, mods, mod_row, g_ffn, wo, wg, wu, wd, g_final, *, final_norm, tm):
    b, l, d = x.shape
    tm = min(tm, l)
    resident = functools.partial(pl.BlockSpec, pipeline_mode=pl.Buffered(1))
    row = lambda w: pl.BlockSpec((1, tm, w), lambda bi, j: (bi, j, 0))
    return pl.pallas_call(
        functools.partial(_ffn_kernel, final_norm=final_norm),
        grid=(b, l // tm),
        in_specs=[
            row(d), row(QW), row(FW), row(QW),
            pl.BlockSpec((1, 6, d), lambda bi, j: (mod_row(bi), 0, 0)),
            pl.BlockSpec((1, d), lambda bi, j: (0, 0)),
            resident(wo.shape, lambda bi, j: (0, 0)),
            resident(wg.shape, lambda bi, j: (0, 0)),
            resident(wu.shape, lambda bi, j: (0, 0)),
            resident(wd.shape, lambda bi, j: (0, 0)),
            pl.BlockSpec((1, d), lambda bi, j: (0, 0)),
        ],
        out_specs=row(d),
        out_shape=jax.ShapeDtypeStruct((b, l, d), F32),
        compiler_params=_cparams(("parallel", "arbitrary")),
        name="out_ffn",
    )(x, og, of, ow, mods, g_ffn, wo, wg, wu, wd, g_final)


def _rope_tables_t(n_tokens):
    pos = np.arange(n_tokens)
    inv_freq = ROPE_THETA ** (-np.arange(0, HALF, 2, dtype=np.float64) / HALF)
    ang = np.concatenate([(pos // GRID_W)[None, :] * inv_freq[:, None],
                          (pos % GRID_W)[None, :] * inv_freq[:, None]], axis=0)
    return jnp.asarray(np.cos(ang), F32), jnp.asarray(np.sin(ang), F32)


def kernel(x, c, ctx, c_ctx, w_ada, b_ada, g_mix, g_ffn, w_in, q_norm, k_norm, sink, w_out, w_gate, w_up, w_down,
           g_final):
    batch, seq, d = x.shape
    n_ctx = ctx.shape[1]
    depth = w_ada.shape[0]
    ctx_row = batch
    rows = -(-(batch + 1) // 8) * 8

    cond = jnp.zeros((rows, d), F32).at[:batch].set(c).at[ctx_row].set(c_ctx)
    mods = _adaln(cond, w_ada, b_ada).reshape(depth * rows, 6, d)

    cos_t, sin_t = _rope_tables_t(seq)
    cos_1, sin_0 = jnp.ones((HALF, n_ctx), F32), jnp.zeros((HALF, n_ctx), F32)

    a_w = QW + 2 * KVW
    w_in_b = w_in.astype(BF16)
    w_t = jnp.concatenate([w_in_b[:, :, :a_w], w_in_b[:, :, a_w + FW:]], axis=2).transpose(0, 2, 1)
    w_n = w_in_b[:, :, a_w:a_w + FW]
    w_out_b, w_gate_b, w_up_b, w_down_b = (w.astype(BF16) for w in (w_out, w_gate, w_up, w_down))
    g_fin = g_final.reshape(1, d)

    for l in range(depth):
        x_row = lambda bi, l=l: l * rows + bi
        c_row = lambda bi, l=l: l * rows + ctx_row
        g_m, g_f = g_mix[l].reshape(1, d), g_ffn[l].reshape(1, d)
        qn, kn = q_norm[l].reshape(HEAD_DIM, 1), k_norm[l].reshape(HEAD_DIM, 1)
        last = l == depth - 1

        qa, ka, va, fb, qc, kc, vc = _mixer_inputs(x, mods, x_row, g_m, w_t[l], w_n[l], qn, kn, cos_t, sin_t, 512)
        qac, kac, vac, fbc, qcc, kcc, vcc = _mixer_inputs(ctx, mods, c_row, g_m, w_t[l], w_n[l], qn, kn,
                                                          cos_1, sin_0, 256)
        og = _attention(qa, kac, vac, ka, va, name="attn_global")
        of = _fourier_mix(fb)
        ow = _attention(qc, kcc, vcc, kc, vc, sink[l], window=True, name="attn_window")
        x = _out_ffn(x, og, of, ow, mods, x_row, g_f, w_out_b[l], w_gate_b[l], w_up_b[l], w_down_b[l], g_fin,
                     final_norm=last, tm=256)
        if not last:
            ogc = _attention(qac, kac, vac, name="attn_ctx_global")
            ofc = _fourier_mix_small(fbc)
            owc = _attention(qcc, kcc, vcc, sink=sink[l], name="attn_ctx_sink")
            ctx = _out_ffn(ctx, ogc, ofc, owc, mods, c_row, g_f, w_out_b[l], w_gate_b[l], w_up_b[l], w_down_b[l],
                           g_fin, final_norm=False, tm=256)
    return x
```

```python
import functools
import math

import jax
import jax.numpy as jnp
import numpy as np
from jax import lax
from jax.experimental import pallas as pl
from jax.experimental.pallas import tpu as pltpu

F32 = jnp.float32
BF16 = jnp.bfloat16

HEAD_DIM = 64
HALF = HEAD_DIM // 2
Q_HEADS = 6
KV_HEADS = 2
GROUP = Q_HEADS // KV_HEADS
QW = Q_HEADS * HEAD_DIM
KVW = KV_HEADS * HEAD_DIM
F_GROUPS = 4
F_CH = 64
FW = F_GROUPS * F_CH
GRID_W = 64
WINDOW = 128
ROPE_THETA = 10000.0
NORM_EPS = 1e-6
NEG_INF = -1e30
LOG2E = math.log2(math.e)
V_ROWS = HEAD_DIM + 16
FFT_L2 = 128
T_WIDTH = 2 * (QW + 2 * KVW)

VMEM_LIMIT = 56 * 1024 * 1024


def _cparams(sem):
    return pltpu.CompilerParams(dimension_semantics=sem, vmem_limit_bytes=VMEM_LIMIT)


def _rms(x, g):
    return x * lax.rsqrt(jnp.mean(x * x, axis=-1, keepdims=True) + NORM_EPS) * g


def _adaln_kernel(cond_ref, w_ref, b_ref, out_ref):
    cnd = cond_ref[...]
    a = cnd / (1.0 + jnp.exp(-cnd))
    out_ref[0] = jnp.dot(a, w_ref[0], preferred_element_type=F32, precision=lax.Precision.HIGHEST) + b_ref[0]


def _adaln(cond, w_ada, b_ada):
    depth, d, n6 = w_ada.shape
    rows = cond.shape[0]
    tn = 768
    return pl.pallas_call(
        _adaln_kernel,
        grid=(depth, n6 // tn),
        in_specs=[
            pl.BlockSpec((rows, d), lambda l, j: (0, 0)),
            pl.BlockSpec((1, d, tn), lambda l, j: (l, 0, j)),
            pl.BlockSpec((1, 1, tn), lambda l, j: (l, 0, j)),
        ],
        out_specs=pl.BlockSpec((1, rows, tn), lambda l, j: (l, 0, j)),
        out_shape=jax.ShapeDtypeStruct((depth, rows, n6), F32),
        compiler_params=_cparams(("arbitrary", "arbitrary")),
        name="adaln",
    )(cond, w_ada, b_ada.reshape(depth, 1, n6))


def _rope_t(x, cos, sin):
    x1, x2 = x[:HALF], x[HALF:]
    return jnp.concatenate([x1 * cos - x2 * sin, x2 * cos + x1 * sin], axis=0)


def _head_norm_t(x, g):
    return x * lax.rsqrt(jnp.mean(x * x, axis=0, keepdims=True) + NORM_EPS) * g


def _mixin_kernel(x_ref, mods_ref, g_ref, wt_ref, wn_ref, qn_ref, kn_ref, cos_ref, sin_ref,
                  qa_ref, ka_ref, va_ref, fb_ref, qc_ref, kc_ref, vc_ref):
    tm = x_ref.shape[1]
    sh1, sc1 = mods_ref[0, 0:1, :], mods_ref[0, 1:2, :]
    h = (_rms(x_ref[0], g_ref[...]) * (1.0 + sc1) + sh1).astype(BF16)
    yt = lax.dot_general(wt_ref[...], h, (((1,), (1,)), ((), ())), preferred_element_type=F32)
    fb_ref[0] = jnp.dot(h, wn_ref[...], preferred_element_type=F32).astype(BF16)
    cos, sin = cos_ref[...], sin_ref[...]
    qn = jnp.broadcast_to(qn_ref[...], (HEAD_DIM, tm))
    kn = jnp.broadcast_to(kn_ref[...], (HEAD_DIM, tm))
    ones = jnp.ones((V_ROWS - HEAD_DIM, tm), BF16)
    scale = HEAD_DIM ** -0.5 * LOG2E

    def head(base, i):
        return yt[base + i * HEAD_DIM: base + (i + 1) * HEAD_DIM, :]

    base = 0
    for i in range(Q_HEADS):
        q = _rope_t(_head_norm_t(head(base, i), qn), cos, sin) * scale
        qa_ref[0, i * HEAD_DIM:(i + 1) * HEAD_DIM, :] = q.astype(BF16)
    base += QW
    k = jnp.concatenate([_rope_t(_head_norm_t(head(base, i), kn), cos, sin) for i in range(KV_HEADS)], axis=0)
    ka_ref[0] = k.T.astype(BF16)
    base += KVW
    for i in range(KV_HEADS):
        va_ref[0, i, :HEAD_DIM, :] = head(base, i).astype(BF16)
        va_ref[0, i, HEAD_DIM:, :] = ones
    base += KVW
    for i in range(Q_HEADS):
        q = _rope_t(head(base, i), cos, sin) * scale
        qc_ref[0, i * HEAD_DIM:(i + 1) * HEAD_DIM, :] = q.astype(BF16)
    base += QW
    k = jnp.concatenate([_rope_t(head(base, i), cos, sin) for i in range(KV_HEADS)], axis=0)
    kc_ref[0] = k.T.astype(BF16)
    base += KVW
    for i in range(KV_HEADS):
        vc_ref[0, i, :HEAD_DIM, :] = head(base, i).astype(BF16)
        vc_ref[0, i, HEAD_DIM:, :] = ones


def _mixer_inputs(x, mods, mod_row, g, wt, wn, qn, kn, cos_t, sin_t, tm):
    b, l, d = x.shape
    tm = min(tm, l)
    bs_qt = pl.BlockSpec((1, QW, tm), lambda bi, j: (bi, 0, j))
    bs_k = pl.BlockSpec((1, tm, KVW), lambda bi, j: (bi, j, 0))
    bs_vt = pl.BlockSpec((1, KV_HEADS, V_ROWS, tm), lambda bi, j: (bi, 0, 0, j))
    sd = jax.ShapeDtypeStruct
    return pl.pallas_call(
        _mixin_kernel,
        grid=(b, l // tm),
        in_specs=[
            pl.BlockSpec((1, tm, d), lambda bi, j: (bi, j, 0)),
            pl.BlockSpec((1, 6, d), lambda bi, j: (mod_row(bi), 0, 0)),
            pl.BlockSpec((1, d), lambda bi, j: (0, 0)),
            pl.BlockSpec(wt.shape, lambda bi, j: (0, 0)),
            pl.BlockSpec(wn.shape, lambda bi, j: (0, 0)),
            pl.BlockSpec((HEAD_DIM, 1), lambda bi, j: (0, 0)),
            pl.BlockSpec((HEAD_DIM, 1), lambda bi, j: (0, 0)),
            pl.BlockSpec((HALF, tm), lambda bi, j: (0, j)),
            pl.BlockSpec((HALF, tm), lambda bi, j: (0, j)),
        ],
        out_specs=[bs_qt, bs_k, bs_vt, pl.BlockSpec((1, tm, FW), lambda bi, j: (bi, j, 0)), bs_qt, bs_k, bs_vt],
        out_shape=[sd((b, QW, l), BF16), sd((b, l, KVW), BF16), sd((b, KV_HEADS, V_ROWS, l), BF16),
                   sd((b, l, FW), BF16),
                   sd((b, QW, l), BF16), sd((b, l, KVW), BF16), sd((b, KV_HEADS, V_ROWS, l), BF16)],
        compiler_params=_cparams(("parallel", "arbitrary")),
        name="mixer_inputs",
    )(x, mods, g, wt, wn, qn, kn, cos_t, sin_t)


def _attn_kernel(*refs, bq, bk, n_main, unroll, window, has_sink, seq):
    refs = list(refs)
    qt_ref, kctx_ref, vctx_ref = refs[:3]
    pos = 3
    if n_main or window:
        k_ref, vt_ref = refs[pos:pos + 2]
        pos += 2
    if has_sink:
        sink_ref = refs[pos]
        pos += 1
    o_ref, ot_ref, qp_ref = refs[pos:pos + 3]
    if n_main:
        acc_ref, m_ref, s_scr, cm_scr = refs[pos + 3:pos + 7]
    n = GROUP * bq
    zeros_pad = jnp.zeros((HEAD_DIM, n), BF16)

    if window:
        band = bq + 2 * WINDOW
        q0 = pl.program_id(1) * bq
        start = pl.multiple_of(jnp.clip(q0 - WINDOW, 0, seq - band), WINDOW)
        kpos = start + lax.broadcasted_iota(jnp.int32, (band, bq), 0)
        qpos = q0 + lax.broadcasted_iota(jnp.int32, (band, bq), 1)
        bias1 = jnp.where(jnp.abs(kpos - qpos) <= WINDOW, 0.0, NEG_INF).astype(F32)
        bias = jnp.concatenate([bias1] * GROUP, axis=1)

    for g in range(KV_HEADS):
        qg = jnp.concatenate([qt_ref[0, (GROUP * g + h) * HEAD_DIM:(GROUP * g + h + 1) * HEAD_DIM, :]
                              for h in range(GROUP)], axis=1)
        qp_ref[...] = jnp.concatenate([qg, zeros_pad] if g == 0 else [zeros_pad, qg], axis=0)

        def logits(keys):
            return jnp.dot(keys, qp_ref[...], preferred_element_type=F32)

        def qk_to(slot, j):
            off = pl.multiple_of(j * bk, bk)
            s = logits(k_ref[0, pl.ds(off, bk), :])
            s_scr[slot] = s
            cm_scr[slot] = jnp.max(s, axis=0, keepdims=True)

        def softmax_pv(slot, j):
            off = pl.multiple_of(j * bk, bk)
            m_old = m_ref[...]
            m_new = jnp.maximum(m_old, cm_scr[slot])
            p = jnp.exp2(s_scr[slot] - m_new).astype(BF16)
            acc_ref[...] = acc_ref[...] * jnp.exp2(m_old - m_new) + jnp.dot(
                vt_ref[0, g, :, pl.ds(off, bk)], p, preferred_element_type=F32)
            m_ref[...] = m_new

        s = logits(kctx_ref[0])
        vt = vctx_ref[0, g]
        if n_main:
            qk_to(0, 0)
        if window:
            s = jnp.concatenate([s, logits(k_ref[0, pl.ds(start, band), :]) + bias], axis=0)
            vt = jnp.concatenate([vt, vt_ref[0, g, :, pl.ds(start, band)]], axis=1)
        m = jnp.max(s, axis=0, keepdims=True)
        if has_sink:
            sink_row = jnp.concatenate(
                [jnp.full((1, bq), sink_ref[GROUP * g + h] * LOG2E, F32) for h in range(GROUP)], axis=1)
            m = jnp.maximum(m, sink_row)
        acc = jnp.dot(vt, jnp.exp2(s - m).astype(BF16), preferred_element_type=F32)
        den = acc[HEAD_DIM:HEAD_DIM + 1]
        if has_sink:
            den = den + jnp.exp2(sink_row - m)

        if n_main:
            m_ref[...] = m
            acc_ref[...] = acc

            def body(i, carry):
                for u in range(unroll):
                    j = unroll * i + u
                    qk_to((u + 1) % 2, j + 1)
                    softmax_pv(u % 2, j)
                return carry
            lax.fori_loop(0, n_main // unroll - 1, body, 0)
            for u in range(unroll):
                j = n_main - unroll + u
                if u < unroll - 1:
                    qk_to((u + 1) % 2, j + 1)
                softmax_pv(u % 2, j)
            acc = acc_ref[...]
            den = acc[HEAD_DIM:HEAD_DIM + 1]

        o = acc[:HEAD_DIM] / den
        for h in range(GROUP):
            ot_ref[(GROUP * g + h) * HEAD_DIM:(GROUP * g + h + 1) * HEAD_DIM, :] = o[:, h * bq:(h + 1) * bq]

    o_ref[0] = ot_ref[...].T.astype(BF16)


def _attention(qt, k_ctx, vt_ctx, k=None, vt=None, sink=None, *, window=False, bq=256, bk=512, unroll=4,
               name="attn"):
    b, _, lq = qt.shape
    c = k_ctx.shape[1]
    bq = min(bq, lq)
    n_main = 0
    seq = 0
    args = [qt, k_ctx, vt_ctx]
    in_specs = [
        pl.BlockSpec((1, QW, bq), lambda bi, i: (bi, 0, i)),
        pl.BlockSpec((1, c, KVW), lambda bi, i: (bi, 0, 0)),
        pl.BlockSpec((1, KV_HEADS, V_ROWS, c), lambda bi, i: (bi, 0, 0, 0)),
    ]
    if k is not None:
        seq = k.shape[1]
        bk = min(bk, seq)
        n_main = 0 if window else seq // bk
        args += [k, vt]
        in_specs += [
            pl.BlockSpec((1, seq, KVW), lambda bi, i: (bi, 0, 0)),
            pl.BlockSpec((1, KV_HEADS, V_ROWS, seq), lambda bi, i: (bi, 0, 0, 0)),
        ]
    if sink is not None:
        args.append(sink)
        in_specs.append(pl.BlockSpec(memory_space=pltpu.SMEM))
    n = GROUP * bq
    scratch = [pltpu.VMEM((QW, bq), F32), pltpu.VMEM((2 * HEAD_DIM, n), BF16)]
    if n_main:
        unroll = min(unroll, n_main)
        assert unroll % 2 == 0 and n_main % unroll == 0 and sink is None
        scratch += [pltpu.VMEM((V_ROWS, n), F32), pltpu.VMEM((1, n), F32),
                    pltpu.VMEM((2, bk, n), F32), pltpu.VMEM((2, 1, n), F32)]
    kern = functools.partial(_attn_kernel, bq=bq, bk=bk, n_main=n_main, unroll=unroll, window=window,
                             has_sink=sink is not None, seq=seq)
    return pl.pallas_call(
        kern,
        grid=(b, lq // bq),
        in_specs=in_specs,
        out_specs=pl.BlockSpec((1, bq, QW), lambda bi, i: (bi, i, 0)),
        out_shape=jax.ShapeDtypeStruct((b, lq, QW), BF16),
        scratch_shapes=scratch,
        compiler_params=_cparams(("parallel", "arbitrary")),
        name=name,
    )(*args)


def _dft_cs(n):
    idx = np.arange(n)
    ang = 2.0 * np.pi * ((idx[:, None] * idx[None, :]) % n) / n
    return np.cos(ang), np.sin(ang)


def _channel_table():
    c, s = _dft_cs(F_CH)
    eye = np.eye(F_GROUPS)
    return np.concatenate([np.kron(eye, c), np.kron(eye, s)], axis=0)


def _fft_stage1_kernel(w_ref, x_ref, y_ref):
    y_ref[0] = jnp.dot(w_ref[...], x_ref[0], preferred_element_type=F32).astype(BF16)


def _fft_stage2_kernel(t_ref, cs_ref, y_ref, o_ref, *, kc, scale):
    for kk in range(kc):
        a = jnp.concatenate([y_ref[0, 0, kk], y_ref[0, 1, kk]], axis=0)
        z = jnp.dot(t_ref[kk], a, preferred_element_type=F32)
        zc = jnp.concatenate([z[:FFT_L2], z[FFT_L2:]], axis=1).astype(BF16)
        o = jnp.dot(zc, cs_ref[...], preferred_element_type=F32) * scale
        o_ref[0, :, kk * FW:(kk + 1) * FW] = o.astype(BF16)


def _fourier_mix(u):
    b, l, _ = u.shape
    l1 = l // FFT_L2
    c1, s1 = _dft_cs(l1)
    w1 = jnp.asarray(np.concatenate([c1, -s1], axis=0), BF16)
    k = (np.arange(l1)[:, None, None] + l1 * np.arange(FFT_L2)[None, :, None])
    ang = 2.0 * np.pi * ((k * np.arange(FFT_L2)[None, None, :]) % l) / l
    mc, ms = np.cos(ang), np.sin(ang)
    t2 = jnp.asarray(np.concatenate([np.concatenate([mc, ms], axis=2),
                                     np.concatenate([-ms, mc], axis=2)], axis=1), BF16)
    cs = jnp.asarray(_channel_table(), BF16)

    cols = FFT_L2 * FW
    tc = min(4096, cols)
    y = pl.pallas_call(
        _fft_stage1_kernel,
        grid=(b, cols // tc),
        in_specs=[pl.BlockSpec((2 * l1, l1), lambda bi, j: (0, 0)),
                  pl.BlockSpec((1, l1, tc), lambda bi, j: (bi, 0, j))],
        out_specs=pl.BlockSpec((1, 2 * l1, tc), lambda bi, j: (bi, 0, j)),
        out_shape=jax.ShapeDtypeStruct((b, 2 * l1, cols), BF16),
        compiler_params=_cparams(("parallel", "arbitrary")),
        name="fft_stage1",
    )(w1, u.reshape(b, l1, cols))

    kc = min(8, l1)
    scale = 1.0 / math.sqrt(l * F_CH)
    out = pl.pallas_call(
        functools.partial(_fft_stage2_kernel, kc=kc, scale=scale),
        grid=(l1 // kc, b),
        in_specs=[pl.BlockSpec((kc, 2 * FFT_L2, 2 * FFT_L2), lambda j, bi: (j, 0, 0)),
                  pl.BlockSpec((2 * FW, FW), lambda j, bi: (0, 0)),
                  pl.BlockSpec((1, 2, kc, FFT_L2, FW), lambda j, bi: (bi, 0, j, 0, 0))],
        out_specs=pl.BlockSpec((1, FFT_L2, kc * FW), lambda j, bi: (bi, 0, j)),
        out_shape=jax.ShapeDtypeStruct((b, FFT_L2, l1 * FW), BF16),
        compiler_params=_cparams(("arbitrary", "arbitrary")),
        name="fft_stage2",
    )(t2, cs, y.reshape(b, 2, l1, FFT_L2, FW))
    return out.reshape(b, l, FW)


def _fourier_small_kernel(wl_ref, cst_ref, u_ref, o_ref, *, scale):
    p = jnp.dot(u_ref[0], cst_ref[...], preferred_element_type=F32)
    pp = jnp.concatenate([p[:, :FW], p[:, FW:]], axis=0).astype(BF16)
    o_ref[0] = (jnp.dot(wl_ref[...], pp, preferred_element_type=F32) * scale).astype(BF16)


def _fourier_mix_small(u):
    b, l, _ = u.shape
    cl, sl = _dft_cs(l)
    wl = jnp.asarray(np.concatenate([cl, -sl], axis=1), BF16)
    cst = jnp.asarray(_channel_table().reshape(2, FW, FW).transpose(1, 0, 2).reshape(FW, 2 * FW), BF16)
    return pl.pallas_call(
        functools.partial(_fourier_small_kernel, scale=1.0 / math.sqrt(l * F_CH)),
        grid=(b,),
        in_specs=[pl.BlockSpec((l, 2 * l), lambda bi: (0, 0)),
                  pl.BlockSpec((FW, 2 * FW), lambda bi: (0, 0)),
                  pl.BlockSpec((1, l, FW), lambda bi: (bi, 0, 0))],
        out_specs=pl.BlockSpec((1, l, FW), lambda bi: (bi, 0, 0)),
        out_shape=jax.ShapeDtypeStruct((b, l, FW), BF16),
        compiler_params=_cparams(("parallel",)),
        name="fourier_ctx",
    )(wl, cst, u)


def _ffn_kernel(x_ref, og_ref, of_ref, ow_ref, mods_ref, g_ref, wo_ref, wg_ref, wu_ref, wd_ref, gf_ref, out_ref,
                *, final_norm):
    gt1, sh2 = mods_ref[0, 2:3, :], mods_ref[0, 3:4, :]
    sc2, gt2 = mods_ref[0, 4:5, :], mods_ref[0, 5:6, :]
    o = jnp.concatenate([og_ref[0], of_ref[0], ow_ref[0]], axis=1)
    x1 = x_ref[0] + gt1 * jnp.dot(o, wo_ref[...], preferred_element_type=F32)
    hn = (_rms(x1, g_ref[...]) * (1.0 + sc2) + sh2).astype(BF16)
    gate = jnp.dot(hn, wg_ref[...], preferred_element_type=F32)
    up = jnp.dot(hn, wu_ref[...], preferred_element_type=F32)
    act = (gate / (1.0 + jnp.exp(-gate)) * up).astype(BF16)
    x2 = x1 + gt2 * jnp.dot(act, wd_ref[...], preferred_element_type=F32)
    if final_norm:
        x2 = _rms(x2, gf_ref[...])
    out_ref[0] = x2


def _out_ffn(x, og, of, ow, mods, mod_row, g_ffn, wo, wg, wu, wd, g_final, *, final_norm, tm):
    b, l, d = x.shape
    tm = min(tm, l)
    resident = functools.partial(pl.BlockSpec, pipeline_mode=pl.Buffered(1))
    row = lambda w: pl.BlockSpec((1, tm, w), lambda bi, j: (bi, j, 0))
    return pl.pallas_call(
        functools.partial(_ffn_kernel, final_norm=final_norm),
        grid=(b, l // tm),
        in_specs=[
            row(d), row(QW), row(FW), row(QW),
            pl.BlockSpec((1, 6, d), lambda bi, j: (mod_row(bi), 0, 0)),
            pl.BlockSpec((1, d), lambda bi, j: (0, 0)),
            resident(wo.shape, lambda bi, j: (0, 0)),
            resident(wg.shape, lambda bi, j: (0, 0)),
            resident(wu.shape, lambda bi, j: (0, 0)),
            resident(wd.shape, lambda bi, j: (0, 0)),
            pl.BlockSpec((1, d), lambda bi, j: (0, 0)),
        ],
        out_specs=row(d),
        out_shape=jax.ShapeDtypeStruct((b, l, d), F32),
        compiler_params=_cparams(("parallel", "arbitrary")),
        name="out_ffn",
    )(x, og, of, ow, mods, g_ffn, wo, wg, wu, wd, g_final)


def _rope_tables_t(n_tokens):
    pos = np.arange(n_tokens)
    inv_freq = ROPE_THETA ** (-np.arange(0, HALF, 2, dtype=np.float64) / HALF)
    ang = np.concatenate([(pos // GRID_W)[None, :] * inv_freq[:, None],
                          (pos % GRID_W)[None, :] * inv_freq[:, None]], axis=0)
    return jnp.asarray(np.cos(ang), F32), jnp.asarray(np.sin(ang), F32)


def kernel(x, c, ctx, c_ctx, w_ada, b_ada, g_mix, g_ffn, w_in, q_norm, k_norm, sink, w_out, w_gate, w_up, w_down,
           g_final):
    batch, seq, d = x.shape
    n_ctx = ctx.shape[1]
    depth = w_ada.shape[0]
    ctx_row = batch
    rows = -(-(batch + 1) // 8) * 8

    cond = jnp.zeros((rows, d), F32).at[:batch].set(c).at[ctx_row].set(c_ctx)
    mods = _adaln(cond, w_ada, b_ada).reshape(depth * rows, 6, d)

    cos_t, sin_t = _rope_tables_t(seq)
    cos_1, sin_0 = jnp.ones((HALF, n_ctx), F32), jnp.zeros((HALF, n_ctx), F32)

    a_w = QW + 2 * KVW
    w_in_b = w_in.astype(BF16)
    w_t = jnp.concatenate([w_in_b[:, :, :a_w], w_in_b[:, :, a_w + FW:]], axis=2).transpose(0, 2, 1)
    w_n = w_in_b[:, :, a_w:a_w + FW]
    w_out_b, w_gate_b, w_up_b, w_down_b = (w.astype(BF16) for w in (w_out, w_gate, w_up, w_down))
    g_fin = g_final.reshape(1, d)

    for l in range(depth):
        x_row = lambda bi, l=l: l * rows + bi
        c_row = lambda bi, l=l: l * rows + ctx_row
        g_m, g_f = g_mix[l].reshape(1, d), g_ffn[l].reshape(1, d)
        qn, kn = q_norm[l].reshape(HEAD_DIM, 1), k_norm[l].reshape(HEAD_DIM, 1)
        last = l == depth - 1

        qa, ka, va, fb, qc, kc, vc = _mixer_inputs(x, mods, x_row, g_m, w_t[l], w_n[l], qn, kn, cos_t, sin_t, 512)
        qac, kac, vac, fbc, qcc, kcc, vcc = _mixer_inputs(ctx, mods, c_row, g_m, w_t[l], w_n[l], qn, kn,
                                                          cos_1, sin_0, 256)
        og = _attention(qa, kac, vac, ka, va, bk=1024, unroll=2, name="attn_global")
        of = _fourier_mix(fb)
        ow = _attention(qc, kcc, vcc, kc, vc, sink[l], window=True, name="attn_window")
        x = _out_ffn(x, og, of, ow, mods, x_row, g_f, w_out_b[l], w_gate_b[l], w_up_b[l], w_down_b[l], g_fin,
                     final_norm=last, tm=256)
        if not last:
            ogc = _attention(qac, kac, vac, name="attn_ctx_global")
            ofc = _fourier_mix_small(fbc)
            owc = _attention(qcc, kcc, vcc, sink=sink[l], name="attn_ctx_sink")
            ctx = _out_ffn(ctx, ogc, ofc, owc, mods, c_row, g_f, w_out_b[l], w_gate_b[l], w_up_b[l], w_down_b[l],
                           g_fin, final_norm=False, tm=256)
    return x
```

```python
import functools
import math

import jax
import jax.numpy as jnp
import numpy as np
from jax import lax
from jax.experimental import pallas as pl
from jax.experimental.pallas import tpu as pltpu

F32 = jnp.float32
BF16 = jnp.bfloat16

HEAD_DIM = 64
HALF = HEAD_DIM // 2
Q_HEADS = 6
KV_HEADS = 2
GROUP = Q_HEADS // KV_HEADS
QW = Q_HEADS * HEAD_DIM
KVW = KV_HEADS * HEAD_DIM
F_GROUPS = 4
F_CH = 64
FW = F_GROUPS * F_CH
GRID_W = 64
WINDOW = 128
ROPE_THETA = 10000.0
NORM_EPS = 1e-6
NEG_INF = -1e30
LOG2E = math.log2(math.e)
V_ROWS = HEAD_DIM + 16
AHEAD = 2
CHUNK = 256
FFT_L2 = 128
T_WIDTH = 2 * (QW + 2 * KVW)

VMEM_LIMIT = 56 * 1024 * 1024


def _cparams(sem):
    return pltpu.CompilerParams(dimension_semantics=sem, vmem_limit_bytes=VMEM_LIMIT)


def _rms(x, g):
    return x * lax.rsqrt(jnp.mean(x * x, axis=-1, keepdims=True) + NORM_EPS) * g


def _adaln_kernel(cond_ref, w_ref, b_ref, out_ref):
    cnd = cond_ref[...]
    a = cnd / (1.0 + jnp.exp(-cnd))
    out_ref[0] = jnp.dot(a, w_ref[0], preferred_element_type=F32, precision=lax.Precision.HIGHEST) + b_ref[0]


def _adaln(cond, w_ada, b_ada):
    depth, d, n6 = w_ada.shape
    rows = cond.shape[0]
    tn = 768
    return pl.pallas_call(
        _adaln_kernel,
        grid=(depth, n6 // tn),
        in_specs=[
            pl.BlockSpec((rows, d), lambda l, j: (0, 0)),
            pl.BlockSpec((1, d, tn), lambda l, j: (l, 0, j)),
            pl.BlockSpec((1, 1, tn), lambda l, j: (l, 0, j)),
        ],
        out_specs=pl.BlockSpec((1, rows, tn), lambda l, j: (l, 0, j)),
        out_shape=jax.ShapeDtypeStruct((depth, rows, n6), F32),
        compiler_params=_cparams(("arbitrary", "arbitrary")),
        name="adaln",
    )(cond, w_ada, b_ada.reshape(depth, 1, n6))


def _rope_t(x, cos, sin):
    x1, x2 = x[:HALF], x[HALF:]
    return jnp.concatenate([x1 * cos - x2 * sin, x2 * cos + x1 * sin], axis=0)


def _head_norm_t(x, g):
    return x * lax.rsqrt(jnp.mean(x * x, axis=0, keepdims=True) + NORM_EPS) * g


def _mixin_kernel(x_ref, mods_ref, g_ref, wt_ref, wn_ref, qn_ref, kn_ref, cos_ref, sin_ref,
                  qa_ref, ka_ref, va_ref, fb_ref, qc_ref, kc_ref, vc_ref):
    tm = x_ref.shape[1]
    sh1, sc1 = mods_ref[0, 0:1, :], mods_ref[0, 1:2, :]
    h = (_rms(x_ref[0], g_ref[...]) * (1.0 + sc1) + sh1).astype(BF16)
    yt = lax.dot_general(wt_ref[...], h, (((1,), (1,)), ((), ())), preferred_element_type=F32)
    fb_ref[0] = jnp.dot(h, wn_ref[...], preferred_element_type=F32).astype(BF16)
    cos, sin = cos_ref[...], sin_ref[...]
    qn = jnp.broadcast_to(qn_ref[...], (HEAD_DIM, tm))
    kn = jnp.broadcast_to(kn_ref[...], (HEAD_DIM, tm))
    ones = jnp.ones((V_ROWS - HEAD_DIM, tm), BF16)
    scale = HEAD_DIM ** -0.5 * LOG2E

    def head(base, i):
        return yt[base + i * HEAD_DIM: base + (i + 1) * HEAD_DIM, :]

    base = 0
    for i in range(Q_HEADS):
        q = _rope_t(_head_norm_t(head(base, i), qn), cos, sin) * scale
        qa_ref[0, i * HEAD_DIM:(i + 1) * HEAD_DIM, :] = q.astype(BF16)
    base += QW
    k = jnp.concatenate([_rope_t(_head_norm_t(head(base, i), kn), cos, sin) for i in range(KV_HEADS)], axis=0)
    ka_ref[0] = k.T.astype(BF16)
    base += KVW
    for i in range(KV_HEADS):
        va_ref[0, i, :HEAD_DIM, :] = head(base, i).astype(BF16)
        va_ref[0, i, HEAD_DIM:, :] = ones
    base += KVW
    for i in range(Q_HEADS):
        q = _rope_t(head(base, i), cos, sin) * scale
        qc_ref[0, i * HEAD_DIM:(i + 1) * HEAD_DIM, :] = q.astype(BF16)
    base += QW
    k = jnp.concatenate([_rope_t(head(base, i), cos, sin) for i in range(KV_HEADS)], axis=0)
    kc_ref[0] = k.T.astype(BF16)
    base += KVW
    for i in range(KV_HEADS):
        vc_ref[0, i, :HEAD_DIM, :] = head(base, i).astype(BF16)
        vc_ref[0, i, HEAD_DIM:, :] = ones


def _mixer_inputs(x, mods, mod_row, g, wt, wn, qn, kn, cos_t, sin_t, tm):
    b, l, d = x.shape
    tm = min(tm, l)
    bs_qt = pl.BlockSpec((1, QW, tm), lambda bi, j: (bi, 0, j))
    bs_k = pl.BlockSpec((1, tm, KVW), lambda bi, j: (bi, j, 0))
    bs_vt = pl.BlockSpec((1, KV_HEADS, V_ROWS, tm), lambda bi, j: (bi, 0, 0, j))
    sd = jax.ShapeDtypeStruct
    return pl.pallas_call(
        _mixin_kernel,
        grid=(b, l // tm),
        in_specs=[
            pl.BlockSpec((1, tm, d), lambda bi, j: (bi, j, 0)),
            pl.BlockSpec((1, 6, d), lambda bi, j: (mod_row(bi), 0, 0)),
            pl.BlockSpec((1, d), lambda bi, j: (0, 0)),
            pl.BlockSpec(wt.shape, lambda bi, j: (0, 0)),
            pl.BlockSpec(wn.shape, lambda bi, j: (0, 0)),
            pl.BlockSpec((HEAD_DIM, 1), lambda bi, j: (0, 0)),
            pl.BlockSpec((HEAD_DIM, 1), lambda bi, j: (0, 0)),
            pl.BlockSpec((HALF, tm), lambda bi, j: (0, j)),
            pl.BlockSpec((HALF, tm), lambda bi, j: (0, j)),
        ],
        out_specs=[bs_qt, bs_k, bs_vt, pl.BlockSpec((1, tm, FW), lambda bi, j: (bi, j, 0)), bs_qt, bs_k, bs_vt],
        out_shape=[sd((b, QW, l), BF16), sd((b, l, KVW), BF16), sd((b, KV_HEADS, V_ROWS, l), BF16),
                   sd((b, l, FW), BF16),
                   sd((b, QW, l), BF16), sd((b, l, KVW), BF16), sd((b, KV_HEADS, V_ROWS, l), BF16)],
        compiler_params=_cparams(("parallel", "arbitrary")),
        name="mixer_inputs",
    )(x, mods, g, wt, wn, qn, kn, cos_t, sin_t)


def _attn_kernel(*refs, bq, bk, n_main, unroll, window, has_sink, seq):
    refs = list(refs)
    qt_ref, kctx_ref, vctx_ref = refs[:3]
    pos = 3
    if n_main or window:
        k_ref, vt_ref = refs[pos:pos + 2]
        pos += 2
    if has_sink:
        sink_ref = refs[pos]
        pos += 1
    o_ref, ot_ref, qp_ref, f_scr = refs[pos:pos + 4]
    if n_main:
        acc_ref, m_ref, s_scr, cm_scr = refs[pos + 4:pos + 8]
    n = GROUP * bq
    zeros_pad = jnp.zeros((HEAD_DIM, n), BF16)

    if window:
        band = bq + 2 * WINDOW
        q0 = pl.program_id(1) * bq
        start = pl.multiple_of(jnp.clip(q0 - WINDOW, 0, seq - band), WINDOW)
        kpos = start + lax.broadcasted_iota(jnp.int32, (band, bq), 0)
        qpos = q0 + lax.broadcasted_iota(jnp.int32, (band, bq), 1)
        bias1 = jnp.where(jnp.abs(kpos - qpos) <= WINDOW, 0.0, NEG_INF).astype(F32)
        bias = jnp.concatenate([bias1] * GROUP, axis=1)

    def logits(g, keys):
        return jnp.dot(keys, qp_ref[g], preferred_element_type=F32)

    def qk_to(g, slot, j):
        off = pl.multiple_of(j * bk, bk)
        s = logits(g, k_ref[0, pl.ds(off, bk), :])
        s_scr[g, slot] = s
        cm_scr[g, slot] = jnp.max(s, axis=0, keepdims=True)

    def softmax_pv(g, slot, j):
        off = pl.multiple_of(j * bk, bk)
        m_old = m_ref[g]
        m_new = jnp.maximum(m_old, cm_scr[g, slot])
        p = jnp.exp2(s_scr[g, slot] - m_new).astype(BF16)
        acc_ref[g] = acc_ref[g] * jnp.exp2(m_old - m_new) + jnp.dot(
            vt_ref[0, g, :, pl.ds(off, bk)], p, preferred_element_type=F32)
        m_ref[g] = m_new

    n_ctx = kctx_ref.shape[1]
    first_max = []
    for g in range(KV_HEADS):
        qg = jnp.concatenate([qt_ref[0, (GROUP * g + h) * HEAD_DIM:(GROUP * g + h + 1) * HEAD_DIM, :]
                              for h in range(GROUP)], axis=1)
        qp_ref[g] = jnp.concatenate([qg, zeros_pad] if g == 0 else [zeros_pad, qg], axis=0)
        s = logits(g, kctx_ref[0])
        f_scr[g, :n_ctx] = s
        m = jnp.max(s, axis=0, keepdims=True)
        if window:
            s = logits(g, k_ref[0, pl.ds(start, band), :]) + bias
            f_scr[g, n_ctx:] = s
            m = jnp.maximum(m, jnp.max(s, axis=0, keepdims=True))
        first_max.append(m)
        if n_main:
            for a in range(AHEAD):
                qk_to(g, a, a)

    first = []
    for g in range(KV_HEADS):
        vt = vctx_ref[0, g]
        if window:
            vt = jnp.concatenate([vt, vt_ref[0, g, :, pl.ds(start, band)]], axis=1)
        s = f_scr[g]
        m = first_max[g]
        if has_sink:
            sink_row = jnp.concatenate(
                [jnp.full((1, bq), sink_ref[GROUP * g + h] * LOG2E, F32) for h in range(GROUP)], axis=1)
            m = jnp.maximum(m, sink_row)
        acc = jnp.dot(vt, jnp.exp2(s - m).astype(BF16), preferred_element_type=F32)
        den = acc[HEAD_DIM:HEAD_DIM + 1]
        if has_sink:
            den = den + jnp.exp2(sink_row - m)
        if n_main:
            m_ref[g] = m
            acc_ref[g] = acc
        first.append((acc, den))

    if n_main:
        def body(i, carry):
            for u in range(unroll):
                j = unroll * i + u
                for g in range(KV_HEADS):
                    qk_to(g, (u + AHEAD) % unroll, j + AHEAD)
                    softmax_pv(g, u, j)
            return carry
        lax.fori_loop(0, n_main // unroll - 1, body, 0)
        for u in range(unroll):
            j = n_main - unroll + u
            for g in range(KV_HEADS):
                if u + AHEAD < unroll:
                    qk_to(g, u + AHEAD, j + AHEAD)
                softmax_pv(g, u, j)

    for g in range(KV_HEADS):
        if n_main:
            acc = acc_ref[g]
            den = acc[HEAD_DIM:HEAD_DIM + 1]
        else:
            acc, den = first[g]
        o = acc[:HEAD_DIM] / den
        for h in range(GROUP):
            ot_ref[(GROUP * g + h) * HEAD_DIM:(GROUP * g + h + 1) * HEAD_DIM, :] = o[:, h * bq:(h + 1) * bq]

    o_ref[0] = ot_ref[...].T.astype(BF16)


def _attention(qt, k_ctx, vt_ctx, k=None, vt=None, sink=None, *, window=False, bq=256, bk=512, unroll=4,
               name="attn"):
    b, _, lq = qt.shape
    c = k_ctx.shape[1]
    bq = min(bq, lq)
    n_main = 0
    seq = 0
    args = [qt, k_ctx, vt_ctx]
    in_specs = [
        pl.BlockSpec((1, QW, bq), lambda bi, i: (bi, 0, i)),
        pl.BlockSpec((1, c, KVW), lambda bi, i: (bi, 0, 0)),
        pl.BlockSpec((1, KV_HEADS, V_ROWS, c), lambda bi, i: (bi, 0, 0, 0)),
    ]
    if k is not None:
        seq = k.shape[1]
        bk = min(bk, seq)
        n_main = 0 if window else seq // bk
        args += [k, vt]
        in_specs += [
            pl.BlockSpec((1, seq, KVW), lambda bi, i: (bi, 0, 0)),
            pl.BlockSpec((1, KV_HEADS, V_ROWS, seq), lambda bi, i: (bi, 0, 0, 0)),
        ]
    if sink is not None:
        args.append(sink)
        in_specs.append(pl.BlockSpec(memory_space=pltpu.SMEM))
    n = GROUP * bq
    first_keys = c + (bq + 2 * WINDOW if window else 0)
    scratch = [pltpu.VMEM((QW, bq), F32), pltpu.VMEM((KV_HEADS, 2 * HEAD_DIM, n), BF16),
               pltpu.VMEM((KV_HEADS, first_keys, n), F32)]
    if n_main:
        unroll = min(unroll, n_main)
        assert unroll > AHEAD and n_main % unroll == 0 and sink is None
        scratch += [pltpu.VMEM((KV_HEADS, V_ROWS, n), F32), pltpu.VMEM((KV_HEADS, 1, n), F32),
                    pltpu.VMEM((KV_HEADS, unroll, bk, n), F32), pltpu.VMEM((KV_HEADS, unroll, 1, n), F32)]
    kern = functools.partial(_attn_kernel, bq=bq, bk=bk, n_main=n_main, unroll=unroll, window=window,
                             has_sink=sink is not None, seq=seq)
    return pl.pallas_call(
        kern,
        grid=(b, lq // bq),
        in_specs=in_specs,
        out_specs=pl.BlockSpec((1, bq, QW), lambda bi, i: (bi, i, 0)),
        out_shape=jax.ShapeDtypeStruct((b, lq, QW), BF16),
        scratch_shapes=scratch,
        compiler_params=_cparams(("parallel", "arbitrary")),
        name=name,
    )(*args)


def _dft_cs(n):
    idx = np.arange(n)
    ang = 2.0 * np.pi * ((idx[:, None] * idx[None, :]) % n) / n
    return np.cos(ang), np.sin(ang)


def _channel_table():
    c, s = _dft_cs(F_CH)
    eye = np.eye(F_GROUPS)
    return np.concatenate([np.kron(eye, c), np.kron(eye, s)], axis=0)


def _fft_stage1_kernel(w_ref, x_ref, y_ref):
    y_ref[0] = jnp.dot(w_ref[...], x_ref[0], preferred_element_type=F32).astype(BF16)


def _fft_stage2_kernel(t_ref, cs_ref, y_ref, o_ref, *, kc, scale):
    for kk in range(kc):
        a = jnp.concatenate([y_ref[0, 0, kk], y_ref[0, 1, kk]], axis=0)
        z = jnp.dot(t_ref[kk], a, preferred_element_type=F32)
        zc = jnp.concatenate([z[:FFT_L2], z[FFT_L2:]], axis=1).astype(BF16)
        o = jnp.dot(zc, cs_ref[...], preferred_element_type=F32) * scale
        o_ref[0, :, kk * FW:(kk + 1) * FW] = o.astype(BF16)


def _fourier_mix(u):
    b, l, _ = u.shape
    l1 = l // FFT_L2
    c1, s1 = _dft_cs(l1)
    w1 = jnp.asarray(np.concatenate([c1, -s1], axis=0), BF16)
    k = (np.arange(l1)[:, None, None] + l1 * np.arange(FFT_L2)[None, :, None])
    ang = 2.0 * np.pi * ((k * np.arange(FFT_L2)[None, None, :]) % l) / l
    mc, ms = np.cos(ang), np.sin(ang)
    t2 = jnp.asarray(np.concatenate([np.concatenate([mc, ms], axis=2),
                                     np.concatenate([-ms, mc], axis=2)], axis=1), BF16)
    cs = jnp.asarray(_channel_table(), BF16)

    cols = FFT_L2 * FW
    tc = min(4096, cols)
    y = pl.pallas_call(
        _fft_stage1_kernel,
        grid=(b, cols // tc),
        in_specs=[pl.BlockSpec((2 * l1, l1), lambda bi, j: (0, 0)),
                  pl.BlockSpec((1, l1, tc), lambda bi, j: (bi, 0, j))],
        out_specs=pl.BlockSpec((1, 2 * l1, tc), lambda bi, j: (bi, 0, j)),
        out_shape=jax.ShapeDtypeStruct((b, 2 * l1, cols), BF16),
        compiler_params=_cparams(("parallel", "arbitrary")),
        name="fft_stage1",
    )(w1, u.reshape(b, l1, cols))

    kc = min(8, l1)
    scale = 1.0 / math.sqrt(l * F_CH)
    out = pl.pallas_call(
        functools.partial(_fft_stage2_kernel, kc=kc, scale=scale),
        grid=(l1 // kc, b),
        in_specs=[pl.BlockSpec((kc, 2 * FFT_L2, 2 * FFT_L2), lambda j, bi: (j, 0, 0)),
                  pl.BlockSpec((2 * FW, FW), lambda j, bi: (0, 0)),
                  pl.BlockSpec((1, 2, kc, FFT_L2, FW), lambda j, bi: (bi, 0, j, 0, 0))],
        out_specs=pl.BlockSpec((1, FFT_L2, kc * FW), lambda j, bi: (bi, 0, j)),
        out_shape=jax.ShapeDtypeStruct((b, FFT_L2, l1 * FW), BF16),
        compiler_params=_cparams(("arbitrary", "arbitrary")),
        name="fft_stage2",
    )(t2, cs, y.reshape(b, 2, l1, FFT_L2, FW))
    return out.reshape(b, l, FW)


def _fourier_small_kernel(wl_ref, cst_ref, u_ref, o_ref, *, scale):
    p = jnp.dot(u_ref[0], cst_ref[...], preferred_element_type=F32)
    pp = jnp.concatenate([p[:, :FW], p[:, FW:]], axis=0).astype(BF16)
    o_ref[0] = (jnp.dot(wl_ref[...], pp, preferred_element_type=F32) * scale).astype(BF16)


def _fourier_mix_small(u):
    b, l, _ = u.shape
    cl, sl = _dft_cs(l)
    wl = jnp.asarray(np.concatenate([cl, -sl], axis=1), BF16)
    cst = jnp.asarray(_channel_table().reshape(2, FW, FW).transpose(1, 0, 2).reshape(FW, 2 * FW), BF16)
    return pl.pallas_call(
        functools.partial(_fourier_small_kernel, scale=1.0 / math.sqrt(l * F_CH)),
        grid=(b,),
        in_specs=[pl.BlockSpec((l, 2 * l), lambda bi: (0, 0)),
                  pl.BlockSpec((FW, 2 * FW), lambda bi: (0, 0)),
                  pl.BlockSpec((1, l, FW), lambda bi: (bi, 0, 0))],
        out_specs=pl.BlockSpec((1, l, FW), lambda bi: (bi, 0, 0)),
        out_shape=jax.ShapeDtypeStruct((b, l, FW), BF16),
        compiler_params=_cparams(("parallel",)),
        name="fourier_ctx",
    )(wl, cst, u)


def _ffn_kernel(x_ref, og_ref, of_ref, ow_ref, mods_ref, g_ref, wo_ref, wg_ref, wu_ref, wd_ref, gf_ref, out_ref,
                *, final_norm):
    gt1, sh2 = mods_ref[0, 2:3, :], mods_ref[0, 3:4, :]
    sc2, gt2 = mods_ref[0, 4:5, :], mods_ref[0, 5:6, :]
    o = jnp.concatenate([og_ref[0], of_ref[0], ow_ref[0]], axis=1)
    x1 = x_ref[0] + gt1 * jnp.dot(o, wo_ref[...], preferred_element_type=F32)
    hn = (_rms(x1, g_ref[...]) * (1.0 + sc2) + sh2).astype(BF16)
    gate = jnp.dot(hn, wg_ref[...], preferred_element_type=F32)
    up = jnp.dot(hn, wu_ref[...], preferred_element_type=F32)
    act = (gate / (1.0 + jnp.exp(-gate)) * up).astype(BF16)
    x2 = x1 + gt2 * jnp.dot(act, wd_ref[...], preferred_element_type=F32)
    if final_norm:
        x2 = _rms(x2, gf_ref[...])
    out_ref[0] = x2


def _out_ffn(x, og, of, ow, mods, mod_row, g_ffn, wo, wg, wu, wd, g_final, *, final_norm, tm):
    b, l, d = x.shape
    tm = min(tm, l)
    resident = functools.partial(pl.BlockSpec, pipeline_mode=pl.Buffered(1))
    row = lambda w: pl.BlockSpec((1, tm, w), lambda bi, j: (bi, j, 0))
    return pl.pallas_call(
        functools.partial(_ffn_kernel, final_norm=final_norm),
        grid=(b, l // tm),
        in_specs=[
            row(d), row(QW), row(FW), row(QW),
            pl.BlockSpec((1, 6, d), lambda bi, j: (mod_row(bi), 0, 0)),
            pl.BlockSpec((1, d), lambda bi, j: (0, 0)),
            resident(wo.shape, lambda bi, j: (0, 0)),
            resident(wg.shape, lambda bi, j: (0, 0)),
            resident(wu.shape, lambda bi, j: (0, 0)),
            resident(wd.shape, lambda bi, j: (0, 0)),
            pl.BlockSpec((1, d), lambda bi, j: (0, 0)),
        ],
        out_specs=row(d),
        out_shape=jax.ShapeDtypeStruct((b, l, d), F32),
        compiler_params=_cparams(("parallel", "arbitrary")),
        name="out_ffn",
    )(x, og, of, ow, mods, g_ffn, wo, wg, wu, wd, g_final)


def _rope_tables_t(n_tokens):
    pos = np.arange(n_tokens)
    inv_freq = ROPE_THETA ** (-np.arange(0, HALF, 2, dtype=np.float64) / HALF)
    ang = np.concatenate([(pos // GRID_W)[None, :] * inv_freq[:, None],
                          (pos % GRID_W)[None, :] * inv_freq[:, None]], axis=0)
    return jnp.asarray(np.cos(ang), F32), jnp.asarray(np.sin(ang), F32)


def kernel(x, c, ctx, c_ctx, w_ada, b_ada, g_mix, g_ffn, w_in, q_norm, k_norm, sink, w_out, w_gate, w_up, w_down,
           g_final):
    batch, seq, d = x.shape
    n_ctx = ctx.shape[1]
    depth = w_ada.shape[0]
    ctx_row = batch
    rows = -(-(batch + 1) // 8) * 8

    cond = jnp.zeros((rows, d), F32).at[:batch].set(c).at[ctx_row].set(c_ctx)
    mods = _adaln(cond, w_ada, b_ada).reshape(depth * rows, 6, d)

    cos_t, sin_t = _rope_tables_t(seq)
    cos_1, sin_0 = jnp.ones((HALF, n_ctx), F32), jnp.zeros((HALF, n_ctx), F32)

    a_w = QW + 2 * KVW
    w_in_b = w_in.astype(BF16)
    w_t = jnp.concatenate([w_in_b[:, :, :a_w], w_in_b[:, :, a_w + FW:]], axis=2).transpose(0, 2, 1)
    w_n = w_in_b[:, :, a_w:a_w + FW]
    w_out_b, w_gate_b, w_up_b, w_down_b = (w.astype(BF16) for w in (w_out, w_gate, w_up, w_down))
    g_fin = g_final.reshape(1, d)

    for l in range(depth):
        x_row = lambda bi, l=l: l * rows + bi
        c_row = lambda bi, l=l: l * rows + ctx_row
        g_m, g_f = g_mix[l].reshape(1, d), g_ffn[l].reshape(1, d)
        qn, kn = q_norm[l].reshape(HEAD_DIM, 1), k_norm[l].reshape(HEAD_DIM, 1)
        last = l == depth - 1

        qa, ka, va, fb, qc, kc, vc = _mixer_inputs(x, mods, x_row, g_m, w_t[l], w_n[l], qn, kn, cos_t, sin_t, 512)
        qac, kac, vac, fbc, qcc, kcc, vcc = _mixer_inputs(ctx, mods, c_row, g_m, w_t[l], w_n[l], qn, kn,
                                                          cos_1, sin_0, 256)
        og = _attention(qa, kac, vac, ka, va, bk=512, unroll=4, name="attn_global")
        of = _fourier_mix(fb)
        ow = _attention(qc, kcc, vcc, kc, vc, sink[l], window=True, name="attn_window")
        x = _out_ffn(x, og, of, ow, mods, x_row, g_f, w_out_b[l], w_gate_b[l], w_up_b[l], w_down_b[l], g_fin,
                     final_norm=last, tm=256)
        if not last:
            ogc = _attention(qac, kac, vac, name="attn_ctx_global")
            ofc = _fourier_mix_small(fbc)
            owc = _attention(qcc, kcc, vcc, sink=sink[l], name="attn_ctx_sink")
            ctx = _out_ffn(ctx, ogc, ofc, owc, mods, c_row, g_f, w_out_b[l], w_gate_b[l], w_up_b[l], w_down_b[l],
                           g_fin, final_norm=False, tm=256)
    return x
```

```python
import functools
import math

import jax
import jax.numpy as jnp
import numpy as np
from jax import lax
from jax.experimental import pallas as pl
from jax.experimental.pallas import tpu as pltpu

F32 = jnp.float32
BF16 = jnp.bfloat16

HEAD_DIM = 64
HALF = HEAD_DIM // 2
Q_HEADS = 6
KV_HEADS = 2
GROUP = Q_HEADS // KV_HEADS
QW = Q_HEADS * HEAD_DIM
KVW = KV_HEADS * HEAD_DIM
F_GROUPS = 4
F_CH = 64
FW = F_GROUPS * F_CH
GRID_W = 64
WINDOW = 128
ROPE_THETA = 10000.0
NORM_EPS = 1e-6
NEG_INF = -1e30
LOG2E = math.log2(math.e)
V_ROWS = HEAD_DIM + 16
AHEAD = 2
CHUNK = 256
FFT_L2 = 128
T_WIDTH = 2 * (QW + 2 * KVW)

VMEM_LIMIT = 56 * 1024 * 1024


def _cparams(sem):
    return pltpu.CompilerParams(dimension_semantics=sem, vmem_limit_bytes=VMEM_LIMIT)


def _rms(x, g):
    return x * lax.rsqrt(jnp.mean(x * x, axis=-1, keepdims=True) + NORM_EPS) * g


def _adaln_kernel(cond_ref, w_ref, b_ref, out_ref):
    cnd = cond_ref[...]
    a = cnd / (1.0 + jnp.exp(-cnd))
    out_ref[0] = jnp.dot(a, w_ref[0], preferred_element_type=F32, precision=lax.Precision.HIGHEST) + b_ref[0]


def _adaln(cond, w_ada, b_ada):
    depth, d, n6 = w_ada.shape
    rows = cond.shape[0]
    tn = 768
    return pl.pallas_call(
        _adaln_kernel,
        grid=(depth, n6 // tn),
        in_specs=[
            pl.BlockSpec((rows, d), lambda l, j: (0, 0)),
            pl.BlockSpec((1, d, tn), lambda l, j: (l, 0, j)),
            pl.BlockSpec((1, 1, tn), lambda l, j: (l, 0, j)),
        ],
        out_specs=pl.BlockSpec((1, rows, tn), lambda l, j: (l, 0, j)),
        out_shape=jax.ShapeDtypeStruct((depth, rows, n6), F32),
        compiler_params=_cparams(("arbitrary", "arbitrary")),
        name="adaln",
    )(cond, w_ada, b_ada.reshape(depth, 1, n6))


def _rope_t(x, cos, sin):
    x1, x2 = x[:HALF], x[HALF:]
    return jnp.concatenate([x1 * cos - x2 * sin, x2 * cos + x1 * sin], axis=0)


def _head_norm_t(x, g):
    return x * lax.rsqrt(jnp.mean(x * x, axis=0, keepdims=True) + NORM_EPS) * g


def _mixin_kernel(x_ref, mods_ref, g_ref, wt_ref, wn_ref, qn_ref, kn_ref, cos_ref, sin_ref,
                  qa_ref, ka_ref, va_ref, fb_ref, qc_ref, kc_ref, vc_ref):
    tm = x_ref.shape[1]
    sh1, sc1 = mods_ref[0, 0:1, :], mods_ref[0, 1:2, :]
    h = (_rms(x_ref[0], g_ref[...]) * (1.0 + sc1) + sh1).astype(BF16)
    yt = lax.dot_general(wt_ref[...], h, (((1,), (1,)), ((), ())), preferred_element_type=F32)
    fb_ref[0] = jnp.dot(h, wn_ref[...], preferred_element_type=F32).astype(BF16)
    cos, sin = cos_ref[...], sin_ref[...]
    qn = jnp.broadcast_to(qn_ref[...], (HEAD_DIM, tm))
    kn = jnp.broadcast_to(kn_ref[...], (HEAD_DIM, tm))
    ones = jnp.ones((V_ROWS - HEAD_DIM, tm), BF16)
    scale = HEAD_DIM ** -0.5 * LOG2E

    def head(base, i):
        return yt[base + i * HEAD_DIM: base + (i + 1) * HEAD_DIM, :]

    base = 0
    for i in range(Q_HEADS):
        q = _rope_t(_head_norm_t(head(base, i), qn), cos, sin) * scale
        qa_ref[0, i * HEAD_DIM:(i + 1) * HEAD_DIM, :] = q.astype(BF16)
    base += QW
    k = jnp.concatenate([_rope_t(_head_norm_t(head(base, i), kn), cos, sin) for i in range(KV_HEADS)], axis=0)
    ka_ref[0] = k.T.astype(BF16)
    base += KVW
    for i in range(KV_HEADS):
        va_ref[0, i, :HEAD_DIM, :] = head(base, i).astype(BF16)
        va_ref[0, i, HEAD_DIM:, :] = ones
    base += KVW
    for i in range(Q_HEADS):
        q = _rope_t(head(base, i), cos, sin) * scale
        qc_ref[0, i * HEAD_DIM:(i + 1) * HEAD_DIM, :] = q.astype(BF16)
    base += QW
    k = jnp.concatenate([_rope_t(head(base, i), cos, sin) for i in range(KV_HEADS)], axis=0)
    kc_ref[0] = k.T.astype(BF16)
    base += KVW
    for i in range(KV_HEADS):
        vc_ref[0, i, :HEAD_DIM, :] = head(base, i).astype(BF16)
        vc_ref[0, i, HEAD_DIM:, :] = ones


def _mixer_inputs(x, mods, mod_row, g, wt, wn, qn, kn, cos_t, sin_t, tm):
    b, l, d = x.shape
    tm = min(tm, l)
    bs_qt = pl.BlockSpec((1, QW, tm), lambda bi, j: (bi, 0, j))
    bs_k = pl.BlockSpec((1, tm, KVW), lambda bi, j: (bi, j, 0))
    bs_vt = pl.BlockSpec((1, KV_HEADS, V_ROWS, tm), lambda bi, j: (bi, 0, 0, j))
    sd = jax.ShapeDtypeStruct
    return pl.pallas_call(
        _mixin_kernel,
        grid=(b, l // tm),
        in_specs=[
            pl.BlockSpec((1, tm, d), lambda bi, j: (bi, j, 0)),
            pl.BlockSpec((1, 6, d), lambda bi, j: (mod_row(bi), 0, 0)),
            pl.BlockSpec((1, d), lambda bi, j: (0, 0)),
            pl.BlockSpec(wt.shape, lambda bi, j: (0, 0)),
            pl.BlockSpec(wn.shape, lambda bi, j: (0, 0)),
            pl.BlockSpec((HEAD_DIM, 1), lambda bi, j: (0, 0)),
            pl.BlockSpec((HEAD_DIM, 1), lambda bi, j: (0, 0)),
            pl.BlockSpec((HALF, tm), lambda bi, j: (0, j)),
            pl.BlockSpec((HALF, tm), lambda bi, j: (0, j)),
        ],
        out_specs=[bs_qt, bs_k, bs_vt, pl.BlockSpec((1, tm, FW), lambda bi, j: (bi, j, 0)), bs_qt, bs_k, bs_vt],
        out_shape=[sd((b, QW, l), BF16), sd((b, l, KVW), BF16), sd((b, KV_HEADS, V_ROWS, l), BF16),
                   sd((b, l, FW), BF16),
                   sd((b, QW, l), BF16), sd((b, l, KVW), BF16), sd((b, KV_HEADS, V_ROWS, l), BF16)],
        compiler_params=_cparams(("parallel", "arbitrary")),
        name="mixer_inputs",
    )(x, mods, g, wt, wn, qn, kn, cos_t, sin_t)


def _attn_kernel(*refs, bq, bk, n_main, unroll, window, has_sink, seq):
    refs = list(refs)
    qt_ref, kctx_ref, vctx_ref = refs[:3]
    pos = 3
    if n_main:
        qnext_ref = refs[pos]
        pos += 1
    if n_main or window:
        k_ref, vt_ref = refs[pos:pos + 2]
        pos += 2
    if has_sink:
        sink_ref = refs[pos]
        pos += 1
    o_ref, ot_ref, qp_ref, f_scr, fm_scr = refs[pos:pos + 5]
    if n_main:
        acc_ref, m_ref, s_scr, cm_scr = refs[pos + 5:pos + 9]
    n = GROUP * bq
    n_ctx = kctx_ref.shape[1]
    zeros_pad = jnp.zeros((HEAD_DIM, n), BF16)

    if window:
        band = bq + 2 * WINDOW
        q0 = pl.program_id(1) * bq
        start = pl.multiple_of(jnp.clip(q0 - WINDOW, 0, seq - band), WINDOW)
        kpos = start + lax.broadcasted_iota(jnp.int32, (band, bq), 0)
        qpos = q0 + lax.broadcasted_iota(jnp.int32, (band, bq), 1)
        bias1 = jnp.where(jnp.abs(kpos - qpos) <= WINDOW, 0.0, NEG_INF).astype(F32)
        bias = jnp.concatenate([bias1] * GROUP, axis=1)

    cur = pl.program_id(1) % 2 if n_main else 0
    nxt = 1 - cur

    def logits(which, g, keys):
        return jnp.dot(keys, qp_ref[which, g], preferred_element_type=F32)

    def pad_queries(q_ref, which):
        for g in range(KV_HEADS):
            qg = jnp.concatenate([q_ref[0, (GROUP * g + h) * HEAD_DIM:(GROUP * g + h + 1) * HEAD_DIM, :]
                                  for h in range(GROUP)], axis=1)
            qp_ref[which, g] = jnp.concatenate([qg, zeros_pad] if g == 0 else [zeros_pad, qg], axis=0)

    def first_logits(which, g):
        s = logits(which, g, kctx_ref[0])
        f_scr[g, :n_ctx] = s
        m = jnp.max(s, axis=0, keepdims=True)
        if window:
            s = logits(which, g, k_ref[0, pl.ds(start, band), :]) + bias
            f_scr[g, n_ctx:] = s
            m = jnp.maximum(m, jnp.max(s, axis=0, keepdims=True))
        fm_scr[g] = m

    def qk_to(which, g, slot, j):
        off = pl.multiple_of(j * bk, bk)
        s = logits(which, g, k_ref[0, pl.ds(off, bk), :])
        s_scr[g, slot] = s
        cm_scr[g, slot] = jnp.max(s, axis=0, keepdims=True)

    def softmax_pv(g, slot, j):
        off = pl.multiple_of(j * bk, bk)
        m_old = m_ref[g]
        m_new = jnp.maximum(m_old, cm_scr[g, slot])
        p = jnp.exp2(s_scr[g, slot] - m_new).astype(BF16)
        acc_ref[g] = acc_ref[g] * jnp.exp2(m_old - m_new) + jnp.dot(
            vt_ref[0, g, :, pl.ds(off, bk)], p, preferred_element_type=F32)
        m_ref[g] = m_new

    def stage_first():
        pad_queries(qt_ref, cur)
        for g in range(KV_HEADS):
            first_logits(cur, g)
            if n_main:
                for a in range(AHEAD):
                    qk_to(cur, g, a, a)

    if n_main:
        pl.when(pl.program_id(1) == 0)(stage_first)
    else:
        stage_first()

    first = []
    for g in range(KV_HEADS):
        vt = vctx_ref[0, g]
        if window:
            vt = jnp.concatenate([vt, vt_ref[0, g, :, pl.ds(start, band)]], axis=1)
        m = fm_scr[g]
        if has_sink:
            sink_row = jnp.concatenate(
                [jnp.full((1, bq), sink_ref[GROUP * g + h] * LOG2E, F32) for h in range(GROUP)], axis=1)
            m = jnp.maximum(m, sink_row)
        acc = jnp.dot(vt, jnp.exp2(f_scr[g] - m).astype(BF16), preferred_element_type=F32)
        den = acc[HEAD_DIM:HEAD_DIM + 1]
        if has_sink:
            den = den + jnp.exp2(sink_row - m)
        if n_main:
            m_ref[g] = m
            acc_ref[g] = acc
        first.append((acc, den))

    if n_main:
        pad_queries(qnext_ref, nxt)
        trips = n_main // unroll

        def body(i, carry):
            for u in range(unroll):
                j = unroll * i + u
                if u + AHEAD < unroll:
                    which, jn = cur, j + AHEAD
                else:
                    wrap = i == trips - 1
                    which = jnp.where(wrap, nxt, cur)
                    jn = jnp.where(wrap, u + AHEAD - unroll, j + AHEAD)
                for g in range(KV_HEADS):
                    qk_to(which, g, (u + AHEAD) % unroll, jn)
                    softmax_pv(g, u, j)
            return carry
        lax.fori_loop(0, trips, body, 0)
        for g in range(KV_HEADS):
            first_logits(nxt, g)

    for g in range(KV_HEADS):
        if n_main:
            acc = acc_ref[g]
            den = acc[HEAD_DIM:HEAD_DIM + 1]
        else:
            acc, den = first[g]
        o = acc[:HEAD_DIM] / den
        for h in range(GROUP):
            ot_ref[(GROUP * g + h) * HEAD_DIM:(GROUP * g + h + 1) * HEAD_DIM, :] = o[:, h * bq:(h + 1) * bq]

    o_ref[0] = ot_ref[...].T.astype(BF16)


def _attention(qt, k_ctx, vt_ctx, k=None, vt=None, sink=None, *, window=False, bq=256, bk=512, unroll=4,
               name="attn"):
    b, _, lq = qt.shape
    c = k_ctx.shape[1]
    bq = min(bq, lq)
    nq = lq // bq
    n_main = 0
    seq = 0
    args = [qt, k_ctx, vt_ctx]
    in_specs = [
        pl.BlockSpec((1, QW, bq), lambda bi, i: (bi, 0, i)),
        pl.BlockSpec((1, c, KVW), lambda bi, i: (bi, 0, 0)),
        pl.BlockSpec((1, KV_HEADS, V_ROWS, c), lambda bi, i: (bi, 0, 0, 0)),
    ]
    if k is not None:
        seq = k.shape[1]
        bk = min(bk, seq)
        n_main = 0 if window else seq // bk
        if n_main:
            args.append(qt)
            in_specs.append(pl.BlockSpec((1, QW, bq), lambda bi, i: (bi, 0, jnp.minimum(i + 1, nq - 1))))
        args += [k, vt]
        in_specs += [
            pl.BlockSpec((1, seq, KVW), lambda bi, i: (bi, 0, 0)),
            pl.BlockSpec((1, KV_HEADS, V_ROWS, seq), lambda bi, i: (bi, 0, 0, 0)),
        ]
    if sink is not None:
        args.append(sink)
        in_specs.append(pl.BlockSpec(memory_space=pltpu.SMEM))
    n = GROUP * bq
    first_keys = c + (bq + 2 * WINDOW if window else 0)
    scratch = [pltpu.VMEM((QW, bq), F32), pltpu.VMEM((2, KV_HEADS, 2 * HEAD_DIM, n), BF16),
               pltpu.VMEM((KV_HEADS, first_keys, n), F32), pltpu.VMEM((KV_HEADS, 1, n), F32)]
    if n_main:
        unroll = min(unroll, n_main)
        assert unroll > AHEAD and n_main % unroll == 0 and sink is None
        scratch += [pltpu.VMEM((KV_HEADS, V_ROWS, n), F32), pltpu.VMEM((KV_HEADS, 1, n), F32),
                    pltpu.VMEM((KV_HEADS, unroll, bk, n), F32), pltpu.VMEM((KV_HEADS, unroll, 1, n), F32)]
    kern = functools.partial(_attn_kernel, bq=bq, bk=bk, n_main=n_main, unroll=unroll, window=window,
                             has_sink=sink is not None, seq=seq)
    return pl.pallas_call(
        kern,
        grid=(b, nq),
        in_specs=in_specs,
        out_specs=pl.BlockSpec((1, bq, QW), lambda bi, i: (bi, i, 0)),
        out_shape=jax.ShapeDtypeStruct((b, lq, QW), BF16),
        scratch_shapes=scratch,
        compiler_params=_cparams(("parallel", "arbitrary")),
        name=name,
    )(*args)


def _dft_cs(n):
    idx = np.arange(n)
    ang = 2.0 * np.pi * ((idx[:, None] * idx[None, :]) % n) / n
    return np.cos(ang), np.sin(ang)


def _channel_table():
    c, s = _dft_cs(F_CH)
    eye = np.eye(F_GROUPS)
    return np.concatenate([np.kron(eye, c), np.kron(eye, s)], axis=0)


def _fft_stage1_kernel(w_ref, x_ref, y_ref):
    y_ref[0] = jnp.dot(w_ref[...], x_ref[0], preferred_element_type=F32).astype(BF16)


def _fft_stage2_kernel(t_ref, cs_ref, y_ref, o_ref, *, kc, scale):
    for kk in range(kc):
        a = jnp.concatenate([y_ref[0, 0, kk], y_ref[0, 1, kk]], axis=0)
        z = jnp.dot(t_ref[kk], a, preferred_element_type=F32)
        zc = jnp.concatenate([z[:FFT_L2], z[FFT_L2:]], axis=1).astype(BF16)
        o = jnp.dot(zc, cs_ref[...], preferred_element_type=F32) * scale
        o_ref[0, :, kk * FW:(kk + 1) * FW] = o.astype(BF16)


def _fourier_mix(u):
    b, l, _ = u.shape
    l1 = l // FFT_L2
    c1, s1 = _dft_cs(l1)
    w1 = jnp.asarray(np.concatenate([c1, -s1], axis=0), BF16)
    k = (np.arange(l1)[:, None, None] + l1 * np.arange(FFT_L2)[None, :, None])
    ang = 2.0 * np.pi * ((k * np.arange(FFT_L2)[None, None, :]) % l) / l
    mc, ms = np.cos(ang), np.sin(ang)
    t2 = jnp.asarray(np.concatenate([np.concatenate([mc, ms], axis=2),
                                     np.concatenate([-ms, mc], axis=2)], axis=1), BF16)
    cs = jnp.asarray(_channel_table(), BF16)

    cols = FFT_L2 * FW
    tc = min(4096, cols)
    y = pl.pallas_call(
        _fft_stage1_kernel,
        grid=(b, cols // tc),
        in_specs=[pl.BlockSpec((2 * l1, l1), lambda bi, j: (0, 0)),
                  pl.BlockSpec((1, l1, tc), lambda bi, j: (bi, 0, j))],
        out_specs=pl.BlockSpec((1, 2 * l1, tc), lambda bi, j: (bi, 0, j)),
        out_shape=jax.ShapeDtypeStruct((b, 2 * l1, cols), BF16),
        compiler_params=_cparams(("parallel", "arbitrary")),
        name="fft_stage1",
    )(w1, u.reshape(b, l1, cols))

    kc = min(8, l1)
    scale = 1.0 / math.sqrt(l * F_CH)
    out = pl.pallas_call(
        functools.partial(_fft_stage2_kernel, kc=kc, scale=scale),
        grid=(l1 // kc, b),
        in_specs=[pl.BlockSpec((kc, 2 * FFT_L2, 2 * FFT_L2), lambda j, bi: (j, 0, 0)),
                  pl.BlockSpec((2 * FW, FW), lambda j, bi: (0, 0)),
                  pl.BlockSpec((1, 2, kc, FFT_L2, FW), lambda j, bi: (bi, 0, j, 0, 0))],
        out_specs=pl.BlockSpec((1, FFT_L2, kc * FW), lambda j, bi: (bi, 0, j)),
        out_shape=jax.ShapeDtypeStruct((b, FFT_L2, l1 * FW), BF16),
        compiler_params=_cparams(("arbitrary", "arbitrary")),
        name="fft_stage2",
    )(t2, cs, y.reshape(b, 2, l1, FFT_L2, FW))
    return out.reshape(b, l, FW)


def _fourier_small_kernel(wl_ref, cst_ref, u_ref, o_ref, *, scale):
    p = jnp.dot(u_ref[0], cst_ref[...], preferred_element_type=F32)
    pp = jnp.concatenate([p[:, :FW], p[:, FW:]], axis=0).astype(BF16)
    o_ref[0] = (jnp.dot(wl_ref[...], pp, preferred_element_type=F32) * scale).astype(BF16)


def _fourier_mix_small(u):
    b, l, _ = u.shape
    cl, sl = _dft_cs(l)
    wl = jnp.asarray(np.concatenate([cl, -sl], axis=1), BF16)
    cst = jnp.asarray(_channel_table().reshape(2, FW, FW).transpose(1, 0, 2).reshape(FW, 2 * FW), BF16)
    return pl.pallas_call(
        functools.partial(_fourier_small_kernel, scale=1.0 / math.sqrt(l * F_CH)),
        grid=(b,),
        in_specs=[pl.BlockSpec((l, 2 * l), lambda bi: (0, 0)),
                  pl.BlockSpec((FW, 2 * FW), lambda bi: (0, 0)),
                  pl.BlockSpec((1, l, FW), lambda bi: (bi, 0, 0))],
        out_specs=pl.BlockSpec((1, l, FW), lambda bi: (bi, 0, 0)),
        out_shape=jax.ShapeDtypeStruct((b, l, FW), BF16),
        compiler_params=_cparams(("parallel",)),
        name="fourier_ctx",
    )(wl, cst, u)


def _ffn_kernel(x_ref, og_ref, of_ref, ow_ref, mods_ref, g_ref, wo_ref, wg_ref, wu_ref, wd_ref, gf_ref, out_ref,
                *, final_norm):
    gt1, sh2 = mods_ref[0, 2:3, :], mods_ref[0, 3:4, :]
    sc2, gt2 = mods_ref[0, 4:5, :], mods_ref[0, 5:6, :]
    o = jnp.concatenate([og_ref[0], of_ref[0], ow_ref[0]], axis=1)
    x1 = x_ref[0] + gt1 * jnp.dot(o, wo_ref[...], preferred_element_type=F32)
    hn = (_rms(x1, g_ref[...]) * (1.0 + sc2) + sh2).astype(BF16)
    gate = jnp.dot(hn, wg_ref[...], preferred_element_type=F32)
    up = jnp.dot(hn, wu_ref[...], preferred_element_type=F32)
    act = (gate / (1.0 + jnp.exp(-gate)) * up).astype(BF16)
    x2 = x1 + gt2 * jnp.dot(act, wd_ref[...], preferred_element_type=F32)
    if final_norm:
        x2 = _rms(x2, gf_ref[...])
    out_ref[0] = x2


def _out_ffn(x, og, of, ow, mods, mod_row, g_ffn, wo, wg, wu, wd, g_final, *, final_norm, tm):
    b, l, d = x.shape
    tm = min(tm, l)
    resident = functools.partial(pl.BlockSpec, pipeline_mode=pl.Buffered(1))
    row = lambda w: pl.BlockSpec((1, tm, w), lambda bi, j: (bi, j, 0))
    return pl.pallas_call(
        functools.partial(_ffn_kernel, final_norm=final_norm),
        grid=(b, l // tm),
        in_specs=[
            row(d), row(QW), row(FW), row(QW),
            pl.BlockSpec((1, 6, d), lambda bi, j: (mod_row(bi), 0, 0)),
            pl.BlockSpec((1, d), lambda bi, j: (0, 0)),
            resident(wo.shape, lambda bi, j: (0, 0)),
            resident(wg.shape, lambda bi, j: (0, 0)),
            resident(wu.shape, lambda bi, j: (0, 0)),
            resident(wd.shape, lambda bi, j: (0, 0)),
            pl.BlockSpec((1, d), lambda bi, j: (0, 0)),
        ],
        out_specs=row(d),
        out_shape=jax.ShapeDtypeStruct((b, l, d), F32),
        compiler_params=_cparams(("parallel", "arbitrary")),
        name="out_ffn",
    )(x, og, of, ow, mods, g_ffn, wo, wg, wu, wd, g_final)


def _rope_tables_t(n_tokens):
    pos = np.arange(n_tokens)
    inv_freq = ROPE_THETA ** (-np.arange(0, HALF, 2, dtype=np.float64) / HALF)
    ang = np.concatenate([(pos // GRID_W)[None, :] * inv_freq[:, None],
                          (pos % GRID_W)[None, :] * inv_freq[:, None]], axis=0)
    return jnp.asarray(np.cos(ang), F32), jnp.asarray(np.sin(ang), F32)


def kernel(x, c, ctx, c_ctx, w_ada, b_ada, g_mix, g_ffn, w_in, q_norm, k_norm, sink, w_out, w_gate, w_up, w_down,
           g_final):
    batch, seq, d = x.shape
    n_ctx = ctx.shape[1]
    depth = w_ada.shape[0]
    ctx_row = batch
    rows = -(-(batch + 1) // 8) * 8

    cond = jnp.zeros((rows, d), F32).at[:batch].set(c).at[ctx_row].set(c_ctx)
    mods = _adaln(cond, w_ada, b_ada).reshape(depth * rows, 6, d)

    cos_t, sin_t = _rope_tables_t(seq)
    cos_1, sin_0 = jnp.ones((HALF, n_ctx), F32), jnp.zeros((HALF, n_ctx), F32)

    a_w = QW + 2 * KVW
    w_in_b = w_in.astype(BF16)
    w_t = jnp.concatenate([w_in_b[:, :, :a_w], w_in_b[:, :, a_w + FW:]], axis=2).transpose(0, 2, 1)
    w_n = w_in_b[:, :, a_w:a_w + FW]
    w_out_b, w_gate_b, w_up_b, w_down_b = (w.astype(BF16) for w in (w_out, w_gate, w_up, w_down))
    g_fin = g_final.reshape(1, d)

    for l in range(depth):
        x_row = lambda bi, l=l: l * rows + bi
        c_row = lambda bi, l=l: l * rows + ctx_row
        g_m, g_f = g_mix[l].reshape(1, d), g_ffn[l].reshape(1, d)
        qn, kn = q_norm[l].reshape(HEAD_DIM, 1), k_norm[l].reshape(HEAD_DIM, 1)
        last = l == depth - 1

        qa, ka, va, fb, qc, kc, vc = _mixer_inputs(x, mods, x_row, g_m, w_t[l], w_n[l], qn, kn, cos_t, sin_t, 1024)
        qac, kac, vac, fbc, qcc, kcc, vcc = _mixer_inputs(ctx, mods, c_row, g_m, w_t[l], w_n[l], qn, kn,
                                                          cos_1, sin_0, 256)
        og = _attention(qa, kac, vac, ka, va, bk=512, unroll=8, name="attn_global")
        of = _fourier_mix(fb)
        ow = _attention(qc, kcc, vcc, kc, vc, sink[l], window=True, name="attn_window")
        x = _out_ffn(x, og, of, ow, mods, x_row, g_f, w_out_b[l], w_gate_b[l], w_up_b[l], w_down_b[l], g_fin,
                     final_norm=last, tm=512)
        if not last:
            ogc = _attention(qac, kac, vac, name="attn_ctx_global")
            ofc = _fourier_mix_small(fbc)
            owc = _attention(qcc, kcc, vcc, sink=sink[l], name="attn_ctx_sink")
            ctx = _out_ffn(ctx, ogc, ofc, owc, mods, c_row, g_f, w_out_b[l], w_gate_b[l], w_up_b[l], w_down_b[l],
                           g_fin, final_norm=False, tm=256)
    return x
```

```python
import functools
import math

import jax
import jax.numpy as jnp
import numpy as np
from jax import lax
from jax.experimental import pallas as pl
from jax.experimental.pallas import tpu as pltpu

F32 = jnp.float32
BF16 = jnp.bfloat16

HEAD_DIM = 64
HALF = HEAD_DIM // 2
Q_HEADS = 6
KV_HEADS = 2
GROUP = Q_HEADS // KV_HEADS
QW = Q_HEADS * HEAD_DIM
KVW = KV_HEADS * HEAD_DIM
F_GROUPS = 4
F_CH = 64
FW = F_GROUPS * F_CH
GRID_W = 64
WINDOW = 128
ROPE_THETA = 10000.0
NORM_EPS = 1e-6
NEG_INF = -1e30
LOG2E = math.log2(math.e)
V_ROWS = HEAD_DIM + 16
AHEAD = 2
FFT_L2 = 128
T_WIDTH = 2 * (QW + 2 * KVW)

V7X_VMEM_BYTES = 64 * 1024 * 1024
VMEM_LIMIT = V7X_VMEM_BYTES - 8 * 1024 * 1024

MIXER_ROWS = 1024
FFN_ROWS = 512
ATTN_BQ = 256
ATTN_BK = 512
ATTN_TRIP = 8
ADALN_COLS = 768
FFT1_COLS = 8192
FFT2_K1 = 16


def _cparams(sem):
    return pltpu.CompilerParams(dimension_semantics=sem, vmem_limit_bytes=VMEM_LIMIT)


def _rms(x, g):
    return x * lax.rsqrt(jnp.mean(x * x, axis=-1, keepdims=True) + NORM_EPS) * g


def _adaln_kernel(cond_ref, w_ref, b_ref, out_ref):
    cnd = cond_ref[...]
    a = cnd / (1.0 + jnp.exp(-cnd))
    out_ref[0] = jnp.dot(a, w_ref[0], preferred_element_type=F32, precision=lax.Precision.HIGHEST) + b_ref[0]


def _adaln(cond, w_ada, b_ada):
    depth, d, n6 = w_ada.shape
    rows = cond.shape[0]
    tn = min(ADALN_COLS, n6)
    return pl.pallas_call(
        _adaln_kernel,
        grid=(depth, n6 // tn),
        in_specs=[
            pl.BlockSpec((rows, d), lambda l, j: (0, 0)),
            pl.BlockSpec((1, d, tn), lambda l, j: (l, 0, j)),
            pl.BlockSpec((1, 1, tn), lambda l, j: (l, 0, j)),
        ],
        out_specs=pl.BlockSpec((1, rows, tn), lambda l, j: (l, 0, j)),
        out_shape=jax.ShapeDtypeStruct((depth, rows, n6), F32),
        compiler_params=_cparams(("arbitrary", "arbitrary")),
        name="adaln",
    )(cond, w_ada, b_ada.reshape(depth, 1, n6))


def _rope_t(x, cos, sin):
    x1, x2 = x[:HALF], x[HALF:]
    return jnp.concatenate([x1 * cos - x2 * sin, x2 * cos + x1 * sin], axis=0)


def _head_norm_t(x, g):
    return x * lax.rsqrt(jnp.mean(x * x, axis=0, keepdims=True) + NORM_EPS) * g


def _mixin_kernel(x_ref, mods_ref, g_ref, wt_ref, wn_ref, qn_ref, kn_ref, cos_ref, sin_ref,
                  qa_ref, ka_ref, va_ref, fb_ref, qc_ref, kc_ref, vc_ref):
    tm = x_ref.shape[1]
    sh1, sc1 = mods_ref[0, 0:1, :], mods_ref[0, 1:2, :]
    h = (_rms(x_ref[0], g_ref[...]) * (1.0 + sc1) + sh1).astype(BF16)
    yt = lax.dot_general(wt_ref[...], h, (((1,), (1,)), ((), ())), preferred_element_type=F32)
    fb_ref[0] = jnp.dot(h, wn_ref[...], preferred_element_type=F32).astype(BF16)
    cos, sin = cos_ref[...], sin_ref[...]
    qn = jnp.broadcast_to(qn_ref[...], (HEAD_DIM, tm))
    kn = jnp.broadcast_to(kn_ref[...], (HEAD_DIM, tm))
    ones = jnp.ones((V_ROWS - HEAD_DIM, tm), BF16)
    scale = HEAD_DIM ** -0.5 * LOG2E

    def head(base, i):
        return yt[base + i * HEAD_DIM: base + (i + 1) * HEAD_DIM, :]

    base = 0
    for i in range(Q_HEADS):
        q = _rope_t(_head_norm_t(head(base, i), qn), cos, sin) * scale
        qa_ref[0, i * HEAD_DIM:(i + 1) * HEAD_DIM, :] = q.astype(BF16)
    base += QW
    k = jnp.concatenate([_rope_t(_head_norm_t(head(base, i), kn), cos, sin) for i in range(KV_HEADS)], axis=0)
    ka_ref[0] = k.T.astype(BF16)
    base += KVW
    for i in range(KV_HEADS):
        va_ref[0, i, :HEAD_DIM, :] = head(base, i).astype(BF16)
        va_ref[0, i, HEAD_DIM:, :] = ones
    base += KVW
    for i in range(Q_HEADS):
        q = _rope_t(head(base, i), cos, sin) * scale
        qc_ref[0, i * HEAD_DIM:(i + 1) * HEAD_DIM, :] = q.astype(BF16)
    base += QW
    k = jnp.concatenate([_rope_t(head(base, i), cos, sin) for i in range(KV_HEADS)], axis=0)
    kc_ref[0] = k.T.astype(BF16)
    base += KVW
    for i in range(KV_HEADS):
        vc_ref[0, i, :HEAD_DIM, :] = head(base, i).astype(BF16)
        vc_ref[0, i, HEAD_DIM:, :] = ones


def _mixer_inputs(x, mods, mod_row, g, wt, wn, qn, kn, cos_t, sin_t):
    b, l, d = x.shape
    tm = min(MIXER_ROWS, l)
    bs_qt =pl.BlockSpec((1, QW, tm), lambda bi, j: (bi, 0, j))
    bs_k = pl.BlockSpec((1, tm, KVW), lambda bi, j: (bi, j, 0))
    bs_vt = pl.BlockSpec((1, KV_HEADS, V_ROWS, tm), lambda bi, j: (bi, 0, 0, j))
    sd = jax.ShapeDtypeStruct
    return pl.pallas_call(
        _mixin_kernel,
        grid=(b, l // tm),
        in_specs=[
            pl.BlockSpec((1, tm, d), lambda bi, j: (bi, j, 0)),
            pl.BlockSpec((1, 6, d), lambda bi, j: (mod_row(bi), 0, 0)),
            pl.BlockSpec((1, d), lambda bi, j: (0, 0)),
            pl.BlockSpec(wt.shape, lambda bi, j: (0, 0)),
            pl.BlockSpec(wn.shape, lambda bi, j: (0, 0)),
            pl.BlockSpec((HEAD_DIM, 1), lambda bi, j: (0, 0)),
            pl.BlockSpec((HEAD_DIM, 1), lambda bi, j: (0, 0)),
            pl.BlockSpec((HALF, tm), lambda bi, j: (0, j)),
            pl.BlockSpec((HALF, tm), lambda bi, j: (0, j)),
        ],
        out_specs=[bs_qt, bs_k, bs_vt, pl.BlockSpec((1, tm, FW), lambda bi, j: (bi, j, 0)), bs_qt, bs_k, bs_vt],
        out_shape=[sd((b, QW, l), BF16), sd((b, l, KVW), BF16), sd((b, KV_HEADS, V_ROWS, l), BF16),
                   sd((b, l, FW), BF16),
                   sd((b, QW, l), BF16), sd((b, l, KVW), BF16), sd((b, KV_HEADS, V_ROWS, l), BF16)],
        compiler_params=_cparams(("parallel", "arbitrary")),
        name="mixer_inputs",
    )(x, mods, g, wt, wn, qn, kn, cos_t, sin_t)


def _attn_kernel(*refs, bq, bk, n_main, unroll, window, has_sink, seq):
    refs = list(refs)
    qt_ref, kctx_ref, vctx_ref = refs[:3]
    pos = 3
    if n_main:
        qnext_ref = refs[pos]
        pos += 1
    if n_main or window:
        k_ref, vt_ref = refs[pos:pos + 2]
        pos += 2
    if has_sink:
        sink_ref = refs[pos]
        pos += 1
    o_ref, ot_ref, qp_ref, f_scr, fm_scr = refs[pos:pos + 5]
    if n_main:
        acc_ref, m_ref, s_scr, cm_scr = refs[pos + 5:pos + 9]
    n = GROUP * bq
    n_ctx = kctx_ref.shape[1]
    zeros_pad = jnp.zeros((HEAD_DIM, n), BF16)

    if window:
        band = bq + 2 * WINDOW
        q0 = pl.program_id(1) * bq
        start = pl.multiple_of(jnp.clip(q0 - WINDOW, 0, seq - band), WINDOW)
        kpos = start + lax.broadcasted_iota(jnp.int32, (band, bq), 0)
        qpos = q0 + lax.broadcasted_iota(jnp.int32, (band, bq), 1)
        bias1 = jnp.where(jnp.abs(kpos - qpos) <= WINDOW, 0.0, NEG_INF).astype(F32)
        bias = jnp.concatenate([bias1] * GROUP, axis=1)

    cur = pl.program_id(1) % 2 if n_main else 0
    nxt = 1 - cur

    def logits(which, g, keys):
        return jnp.dot(keys, qp_ref[which, g], preferred_element_type=F32)

    def pad_queries(q_ref, which, g):
        qg = jnp.concatenate([q_ref[0, (GROUP * g + h) * HEAD_DIM:(GROUP * g + h + 1) * HEAD_DIM, :]
                              for h in range(GROUP)], axis=1)
        qp_ref[which, g] = jnp.concatenate([qg, zeros_pad] if g == 0 else [zeros_pad, qg], axis=0)

    def first_logits(which, g):
        s = logits(which, g, kctx_ref[0])
        f_scr[g, :n_ctx] = s
        m = jnp.max(s, axis=0, keepdims=True)
        if window:
            s = logits(which, g, k_ref[0, pl.ds(start, band), :]) + bias
            f_scr[g, n_ctx:] = s
            m = jnp.maximum(m, jnp.max(s, axis=0, keepdims=True))
        fm_scr[g] = m

    def qk_to(which, g, slot, j):
        off = pl.multiple_of(j * bk, bk)
        s = logits(which, g, k_ref[0, pl.ds(off, bk), :])
        s_scr[g, slot] = s
        cm_scr[g, slot] = jnp.max(s, axis=0, keepdims=True)

    def softmax_pv(g, slot, j):
        off = pl.multiple_of(j * bk, bk)
        m_old = m_ref[g]
        m_new = jnp.maximum(m_old, cm_scr[g, slot])
        p = jnp.exp2(s_scr[g, slot] - m_new).astype(BF16)
        acc_ref[g] = acc_ref[g] * jnp.exp2(m_old - m_new) + jnp.dot(
            vt_ref[0, g, :, pl.ds(off, bk)], p, preferred_element_type=F32)
        m_ref[g] = m_new

    def stage_first():
        for g in range(KV_HEADS):
            pad_queries(qt_ref, cur, g)
            first_logits(cur, g)
            if n_main:
                for a in range(AHEAD):
                    qk_to(cur, g, a, a)

    if n_main:
        pl.when(pl.program_id(1) == 0)(stage_first)
    else:
        stage_first()

    first = []
    for g in range(KV_HEADS):
        vt = vctx_ref[0, g]
        if window:
            vt = jnp.concatenate([vt, vt_ref[0, g, :, pl.ds(start, band)]], axis=1)
        m = fm_scr[g]
        if has_sink:
            sink_row = jnp.concatenate(
                [jnp.full((1, bq), sink_ref[GROUP * g + h] * LOG2E, F32) for h in range(GROUP)], axis=1)
            m = jnp.maximum(m, sink_row)
        acc = jnp.dot(vt, jnp.exp2(f_scr[g] - m).astype(BF16), preferred_element_type=F32)
        den = acc[HEAD_DIM:HEAD_DIM + 1]
        if has_sink:
            den = den + jnp.exp2(sink_row - m)
        if n_main:
            m_ref[g] = m
            acc_ref[g] = acc
        first.append((acc, den))

    if n_main:
        for g in range(KV_HEADS):
            pad_queries(qnext_ref, nxt, g)
        trips = n_main // unroll

        def body(i, carry):
            for u in range(unroll):
                j = unroll * i + u
                if u + AHEAD < unroll:
                    which, jn = cur, j + AHEAD
                else:
                    wrap = i == trips - 1
                    which = jnp.where(wrap, nxt, cur)
                    jn = jnp.where(wrap, u + AHEAD - unroll, j + AHEAD)
                for g in range(KV_HEADS):
                    qk_to(which, g, (u + AHEAD) % unroll, jn)
                    softmax_pv(g, u, j)
            return carry
        lax.fori_loop(0, trips, body, 0)
        for g in range(KV_HEADS):
            first_logits(nxt, g)

    for g in range(KV_HEADS):
        if n_main:
            acc = acc_ref[g]
            den = acc[HEAD_DIM:HEAD_DIM + 1]
        else:
            acc, den = first[g]
        o = acc[:HEAD_DIM] / den
        for h in range(GROUP):
            ot_ref[(GROUP * g + h) * HEAD_DIM:(GROUP * g + h + 1) * HEAD_DIM, :] = o[:, h * bq:(h + 1) * bq]

    o_ref[0] = ot_ref[...].T.astype(BF16)


def _attention(qt, k_ctx, vt_ctx, k=None, vt=None, sink=None, *, window=False, bq=ATTN_BQ, bk=ATTN_BK,
               unroll=ATTN_TRIP, name="attn"):
    b, _, lq = qt.shape
    c = k_ctx.shape[1]
    bq = min(bq, lq)
    nq = lq // bq
    n_main = 0
    seq = 0
    args = [qt, k_ctx, vt_ctx]
    in_specs = [
        pl.BlockSpec((1, QW, bq), lambda bi, i: (bi, 0, i)),
        pl.BlockSpec((1, c, KVW), lambda bi, i: (bi, 0, 0)),
        pl.BlockSpec((1, KV_HEADS, V_ROWS, c), lambda bi, i: (bi, 0, 0, 0)),
    ]
    if k is not None:
        seq = k.shape[1]
        bk = min(bk, seq)
        n_main = 0 if window else seq // bk
        if n_main:
            args.append(qt)
            in_specs.append(pl.BlockSpec((1, QW, bq), lambda bi, i: (bi, 0, jnp.minimum(i + 1, nq - 1))))
        args += [k, vt]
        in_specs += [
            pl.BlockSpec((1, seq, KVW), lambda bi, i: (bi, 0, 0)),
            pl.BlockSpec((1, KV_HEADS, V_ROWS, seq), lambda bi, i: (bi, 0, 0, 0)),
        ]
    if sink is not None:
        args.append(sink)
        in_specs.append(pl.BlockSpec(memory_space=pltpu.SMEM))
    n = GROUP * bq
    first_keys = c + (bq + 2 * WINDOW if window else 0)
    scratch = [pltpu.VMEM((QW, bq), F32), pltpu.VMEM((2, KV_HEADS, 2 * HEAD_DIM, n), BF16),
               pltpu.VMEM((KV_HEADS, first_keys, n), F32), pltpu.VMEM((KV_HEADS, 1, n), F32)]
    if n_main:
        unroll = min(unroll, n_main)
        assert unroll > AHEAD and n_main % unroll == 0 and sink is None
        scratch += [pltpu.VMEM((KV_HEADS, V_ROWS, n), F32), pltpu.VMEM((KV_HEADS, 1, n), F32),
                    pltpu.VMEM((KV_HEADS, unroll, bk, n), F32), pltpu.VMEM((KV_HEADS, unroll, 1, n), F32)]
    kern = functools.partial(_attn_kernel, bq=bq, bk=bk, n_main=n_main, unroll=unroll, window=window,
                             has_sink=sink is not None, seq=seq)
    return pl.pallas_call(
        kern,
        grid=(b, nq),
        in_specs=in_specs,
        out_specs=pl.BlockSpec((1, bq, QW), lambda bi, i: (bi, i, 0)),
        out_shape=jax.ShapeDtypeStruct((b, lq, QW), BF16),
        scratch_shapes=scratch,
        compiler_params=_cparams(("parallel", "arbitrary")),
        name=name,
    )(*args)


def _dft_cs(n):
    idx = np.arange(n)
    ang = 2.0 * np.pi * ((idx[:, None] * idx[None, :]) % n) / n
    return np.cos(ang), np.sin(ang)


def _mxu_table(table):
    return jnp.asarray(table, F32).astype(BF16)


def _channel_table():
    c, s = _dft_cs(F_CH)
    eye = np.eye(F_GROUPS)
    return np.concatenate([np.kron(eye, c), np.kron(eye, s)], axis=0)


def _fft_stage1_kernel(w_ref, x_ref, y_ref):
    y_ref[0] = jnp.dot(w_ref[...], x_ref[0], preferred_element_type=F32).astype(BF16)


def _fft_stage2_kernel(t_ref, cs_ref, y_ref, o_ref, *, kc, scale):
    for kk in range(kc):
        a = jnp.concatenate([y_ref[0, 0, kk], y_ref[0, 1, kk]], axis=0)
        z = jnp.dot(t_ref[kk], a, preferred_element_type=F32)
        zc = jnp.concatenate([z[:FFT_L2], z[FFT_L2:]], axis=1).astype(BF16)
        o = jnp.dot(zc, cs_ref[...], preferred_element_type=F32) * scale
        o_ref[0, :, kk * FW:(kk + 1) * FW] = o.astype(BF16)


def _fourier_mix(u):
    b, l, _ = u.shape
    l1 = l // FFT_L2
    c1, s1 = _dft_cs(l1)
    w1 = _mxu_table(np.concatenate([c1, -s1], axis=0))
    k = (np.arange(l1)[:, None, None] + l1 * np.arange(FFT_L2)[None, :, None])
    ang = 2.0 * np.pi * ((k * np.arange(FFT_L2)[None, None, :]) % l) / l
    mc, ms = np.cos(ang), np.sin(ang)
    t2 = _mxu_table(np.concatenate([np.concatenate([mc, ms], axis=2),
                                    np.concatenate([-ms, mc], axis=2)], axis=1))
    cs = _mxu_table(_channel_table())

    cols = FFT_L2 * FW
    tc = min(FFT1_COLS, cols)
    y = pl.pallas_call(
        _fft_stage1_kernel,
        grid=(b, cols // tc),
        in_specs=[pl.BlockSpec((2 * l1, l1), lambda bi, j: (0, 0)),
                  pl.BlockSpec((1, l1, tc), lambda bi, j: (bi, 0, j))],
        out_specs=pl.BlockSpec((1, 2 * l1, tc), lambda bi, j: (bi, 0, j)),
        out_shape=jax.ShapeDtypeStruct((b, 2 * l1, cols), BF16),
        compiler_params=_cparams(("parallel", "arbitrary")),
        name="fft_stage1",
    )(w1, u.reshape(b, l1, cols))

    kc = min(FFT2_K1, l1)
    scale = 1.0 / math.sqrt(l * F_CH)
    out = pl.pallas_call(
        functools.partial(_fft_stage2_kernel, kc=kc, scale=scale),
        grid=(l1 // kc, b),
        in_specs=[pl.BlockSpec((kc, 2 * FFT_L2, 2 * FFT_L2), lambda j, bi: (j, 0, 0)),
                  pl.BlockSpec((2 * FW, FW), lambda j, bi: (0, 0)),
                  pl.BlockSpec((1, 2, kc, FFT_L2, FW), lambda j, bi: (bi, 0, j, 0, 0))],
        out_specs=pl.BlockSpec((1, FFT_L2, kc * FW), lambda j, bi: (bi, 0, j)),
        out_shape=jax.ShapeDtypeStruct((b, FFT_L2, l1 * FW), BF16),
        compiler_params=_cparams(("arbitrary", "arbitrary")),
        name="fft_stage2",
    )(t2, cs, y.reshape(b, 2, l1, FFT_L2, FW))
    return out.reshape(b, l, FW)


def _fourier_small_kernel(wl_ref, cst_ref, u_ref, o_ref, *, scale):
    p = jnp.dot(u_ref[0], cst_ref[...], preferred_element_type=F32)
    pp = jnp.concatenate([p[:, :FW], p[:, FW:]], axis=0).astype(BF16)
    o_ref[0] = (jnp.dot(wl_ref[...], pp, preferred_element_type=F32) * scale).astype(BF16)


def _fourier_mix_small(u):
    b, l, _ = u.shape
    cl, sl = _dft_cs(l)
    wl = _mxu_table(np.concatenate([cl, -sl], axis=1))
    cst = _mxu_table(_channel_table().reshape(2, FW, FW).transpose(1, 0, 2).reshape(FW, 2 * FW))
    return pl.pallas_call(
        functools.partial(_fourier_small_kernel, scale=1.0 / math.sqrt(l * F_CH)),
        grid=(b,),
        in_specs=[pl.BlockSpec((l, 2 * l), lambda bi: (0, 0)),
                  pl.BlockSpec((FW, 2 * FW), lambda bi: (0, 0)),
                  pl.BlockSpec((1, l, FW), lambda bi: (bi, 0, 0))],
        out_specs=pl.BlockSpec((1, l, FW), lambda bi: (bi, 0, 0)),
        out_shape=jax.ShapeDtypeStruct((b, l, FW), BF16),
        compiler_params=_cparams(("parallel",)),
        name="fourier_ctx",
    )(wl, cst, u)


def _ffn_kernel(x_ref, og_ref, of_ref, ow_ref, mods_ref, g_ref, wo_ref, wg_ref, wu_ref, wd_ref, gf_ref, out_ref,
                *, final_norm):
    gt1, sh2 = mods_ref[0, 2:3, :], mods_ref[0, 3:4, :]
    sc2, gt2 = mods_ref[0, 4:5, :], mods_ref[0, 5:6, :]
    o = jnp.concatenate([og_ref[0], of_ref[0], ow_ref[0]], axis=1)
    x1 = x_ref[0] + gt1 * jnp.dot(o, wo_ref[...], preferred_element_type=F32)
    hn = (_rms(x1, g_ref[...]) * (1.0 + sc2) + sh2).astype(BF16)
    gate = jnp.dot(hn, wg_ref[...], preferred_element_type=F32)
    up = jnp.dot(hn, wu_ref[...], preferred_element_type=F32)
    act = (gate / (1.0 + jnp.exp(-gate)) * up).astype(BF16)
    x2 = x1 + gt2 * jnp.dot(act, wd_ref[...], preferred_element_type=F32)
    if final_norm:
        x2 = _rms(x2, gf_ref[...])
    out_ref[0] = x2


def _out_ffn(x, og, of, ow, mods, mod_row, g_ffn, wo, wg, wu, wd, g_final, *, final_norm):
    b, l, d = x.shape
    tm = min(FFN_ROWS, l)
    resident = functools.partial(pl.BlockSpec, pipeline_mode=pl.Buffered(1))
    row = lambda w: pl.BlockSpec((1, tm, w), lambda bi, j: (bi, j, 0))
    return pl.pallas_call(
        functools.partial(_ffn_kernel, final_norm=final_norm),
        grid=(b, l // tm),
        in_specs=[
            row(d), row(QW), row(FW), row(QW),
            pl.BlockSpec((1, 6, d), lambda bi, j: (mod_row(bi), 0, 0)),
            pl.BlockSpec((1, d), lambda bi, j: (0, 0)),
            resident(wo.shape, lambda bi, j: (0, 0)),
            resident(wg.shape, lambda bi, j: (0, 0)),
            resident(wu.shape, lambda bi, j: (0, 0)),
            resident(wd.shape, lambda bi, j: (0, 0)),
            pl.BlockSpec((1, d), lambda bi, j: (0, 0)),
        ],
        out_specs=row(d),
        out_shape=jax.ShapeDtypeStruct((b, l, d), F32),
        compiler_params=_cparams(("parallel", "arbitrary")),
        name="out_ffn",
    )(x, og, of, ow, mods, g_ffn, wo, wg, wu, wd, g_final)


def _rope_tables_t(n_tokens):
    pos = np.arange(n_tokens)
    inv_freq = ROPE_THETA ** (-np.arange(0, HALF, 2, dtype=np.float64) / HALF)
    ang = np.concatenate([(pos // GRID_W)[None, :] * inv_freq[:, None],
                          (pos % GRID_W)[None, :] * inv_freq[:, None]], axis=0)
    return jnp.asarray(np.cos(ang), F32), jnp.asarray(np.sin(ang), F32)


def kernel(x, c, ctx, c_ctx, w_ada, b_ada, g_mix, g_ffn, w_in, q_norm, k_norm, sink, w_out, w_gate, w_up, w_down,
           g_final):
    batch, seq, d = x.shape
    n_ctx = ctx.shape[1]
    depth = w_ada.shape[0]
    ctx_row = batch
    rows = -(-(batch + 1) // 8) * 8

    cond = jnp.zeros((rows, d), F32).at[:batch].set(c).at[ctx_row].set(c_ctx)
    mods = _adaln(cond, w_ada, b_ada).reshape(depth * rows, 6, d)

    cos_t, sin_t = _rope_tables_t(seq)
    cos_1, sin_0 = jnp.ones((HALF, n_ctx), F32), jnp.zeros((HALF, n_ctx), F32)

    a_w = QW + 2 * KVW
    w_in_b = w_in.astype(BF16)
    w_t = jnp.concatenate([w_in_b[:, :, :a_w], w_in_b[:, :, a_w + FW:]], axis=2).transpose(0, 2, 1)
    w_n = w_in_b[:, :, a_w:a_w + FW]
    w_out_b, w_gate_b, w_up_b, w_down_b = (w.astype(BF16) for w in (w_out, w_gate, w_up, w_down))
    g_fin = g_final.reshape(1, d)

    for l in range(depth):
        x_row = lambda bi, l=l: l * rows + bi
        c_row = lambda bi, l=l: l * rows + ctx_row
        g_m, g_f = g_mix[l].reshape(1, d), g_ffn[l].reshape(1, d)
        qn, kn = q_norm[l].reshape(HEAD_DIM, 1), k_norm[l].reshape(HEAD_DIM, 1)
        last = l == depth - 1

        qa, ka, va, fb, qc, kc, vc = _mixer_inputs(x, mods, x_row, g_m, w_t[l], w_n[l], qn, kn, cos_t, sin_t)
        qac, kac, vac, fbc, qcc, kcc, vcc = _mixer_inputs(ctx, mods, c_row, g_m, w_t[l], w_n[l], qn, kn,
                                                          cos_1, sin_0)
        og = _attention(qa, kac, vac, ka, va, name="attn_global")
        of = _fourier_mix(fb)
        ow = _attention(qc, kcc, vcc, kc, vc, sink[l], window=True, name="attn_window")
        x = _out_ffn(x, og, of, ow, mods, x_row, g_f, w_out_b[l], w_gate_b[l], w_up_b[l], w_down_b[l], g_fin,
                     final_norm=last)
        if not last:
            ogc = _attention(qac, kac, vac, name="attn_ctx_global")
            ofc = _fourier_mix_small(fbc)
            owc = _attention(qcc, kcc, vcc, sink=sink[l], name="attn_ctx_sink")
            ctx = _out_ffn(ctx, ogc, ofc, owc, mods, c_row, g_f, w_out_b[l], w_gate_b[l], w_up_b[l], w_down_b[l],
                           g_fin, final_norm=False)
    return x
```

```python
import functools
import math

import jax
import jax.numpy as jnp
import numpy as np
from jax import lax
from jax.experimental import pallas as pl
from jax.experimental.pallas import tpu as pltpu

F32 = jnp.float32
BF16 = jnp.bfloat16

HEAD_DIM = 64
HALF = HEAD_DIM // 2
Q_HEADS = 6
KV_HEADS = 2
GROUP = Q_HEADS // KV_HEADS
QW = Q_HEADS * HEAD_DIM
KVW = KV_HEADS * HEAD_DIM
F_GROUPS = 4
F_CH = 64
FW = F_GROUPS * F_CH
GRID_W = 64
WINDOW = 128
ROPE_THETA = 10000.0
NORM_EPS = 1e-6
NEG_INF = -1e30
LOG2E = math.log2(math.e)
V_ROWS = HEAD_DIM + 16
AHEAD = 2
FFT_L2 = 128
T_WIDTH = 2 * (QW + 2 * KVW)

V7X_VMEM_BYTES = 64 * 1024 * 1024
VMEM_LIMIT = V7X_VMEM_BYTES - 8 * 1024 * 1024

MIXER_ROWS = 1024
FFN_ROWS = 512
ATTN_BQ = 256
ATTN_BK = 256
ATTN_TRIP = 16
ADALN_COLS = 768
FFT1_COLS = 8192
FFT2_K1 = 16


def _cparams(sem):
    return pltpu.CompilerParams(dimension_semantics=sem, vmem_limit_bytes=VMEM_LIMIT)


def _rms(x, g):
    return x * lax.rsqrt(jnp.mean(x * x, axis=-1, keepdims=True) + NORM_EPS) * g


def _adaln_kernel(cond_ref, w_ref, b_ref, out_ref):
    cnd = cond_ref[...]
    a = cnd / (1.0 + jnp.exp(-cnd))
    out_ref[0] = jnp.dot(a, w_ref[0], preferred_element_type=F32, precision=lax.Precision.HIGHEST) + b_ref[0]


def _adaln(cond, w_ada, b_ada):
    depth, d, n6 = w_ada.shape
    rows = cond.shape[0]
    tn = min(ADALN_COLS, n6)
    return pl.pallas_call(
        _adaln_kernel,
        grid=(depth, n6 // tn),
        in_specs=[
            pl.BlockSpec((rows, d), lambda l, j: (0, 0)),
            pl.BlockSpec((1, d, tn), lambda l, j: (l, 0, j)),
            pl.BlockSpec((1, 1, tn), lambda l, j: (l, 0, j)),
        ],
        out_specs=pl.BlockSpec((1, rows, tn), lambda l, j: (l, 0, j)),
        out_shape=jax.ShapeDtypeStruct((depth, rows, n6), F32),
        compiler_params=_cparams(("arbitrary", "arbitrary")),
        name="adaln",
    )(cond, w_ada, b_ada.reshape(depth, 1, n6))


def _rope_t(x, cos, sin):
    x1, x2 = x[:HALF], x[HALF:]
    return jnp.concatenate([x1 * cos - x2 * sin, x2 * cos + x1 * sin], axis=0)


def _head_norm_t(x, g):
    return x * lax.rsqrt(jnp.mean(x * x, axis=0, keepdims=True) + NORM_EPS) * g


def _mixin_kernel(x_ref, mods_ref, g_ref, wt_ref, wn_ref, qn_ref, kn_ref, cos_ref, sin_ref,
                  qa_ref, ka_ref, va_ref, fb_ref, qc_ref, kc_ref, vc_ref):
    tm = x_ref.shape[1]
    sh1, sc1 = mods_ref[0, 0:1, :], mods_ref[0, 1:2, :]
    h = (_rms(x_ref[0], g_ref[...]) * (1.0 + sc1) + sh1).astype(BF16)
    yt = lax.dot_general(wt_ref[...], h, (((1,), (1,)), ((), ())), preferred_element_type=F32)
    fb_ref[0] = jnp.dot(h, wn_ref[...], preferred_element_type=F32).astype(BF16)
    cos, sin = cos_ref[...], sin_ref[...]
    qn = jnp.broadcast_to(qn_ref[...], (HEAD_DIM, tm))
    kn = jnp.broadcast_to(kn_ref[...], (HEAD_DIM, tm))
    ones = jnp.ones((V_ROWS - HEAD_DIM, tm), BF16)
    scale = HEAD_DIM ** -0.5 * LOG2E

    def head(base, i):
        return yt[base + i * HEAD_DIM: base + (i + 1) * HEAD_DIM, :]

    base = 0
    for i in range(Q_HEADS):
        q = _rope_t(_head_norm_t(head(base, i), qn), cos, sin) * scale
        qa_ref[0, i * HEAD_DIM:(i + 1) * HEAD_DIM, :] = q.astype(BF16)
    base += QW
    k = jnp.concatenate([_rope_t(_head_norm_t(head(base, i), kn), cos, sin) for i in range(KV_HEADS)], axis=0)
    ka_ref[0] = k.T.astype(BF16)
    base += KVW
    for i in range(KV_HEADS):
        va_ref[0, i, :HEAD_DIM, :] = head(base, i).astype(BF16)
        va_ref[0, i, HEAD_DIM:, :] = ones
    base += KVW
    for i in range(Q_HEADS):
        q = _rope_t(head(base, i), cos, sin) * scale
        qc_ref[0, i * HEAD_DIM:(i + 1) * HEAD_DIM, :] = q.astype(BF16)
    base += QW
    k = jnp.concatenate([_rope_t(head(base, i), cos, sin) for i in range(KV_HEADS)], axis=0)
    kc_ref[0] = k.T.astype(BF16)
    base += KVW
    for i in range(KV_HEADS):
        vc_ref[0, i, :HEAD_DIM, :] = head(base, i).astype(BF16)
        vc_ref[0, i, HEAD_DIM:, :] = ones


def _mixer_inputs(x, mods, mod_row, g, wt, wn, qn, kn, cos_t, sin_t):
    b, l, d = x.shape
    tm = min(MIXER_ROWS, l)
    bs_qt =pl.BlockSpec((1, QW, tm), lambda bi, j: (bi, 0, j))
    bs_k = pl.BlockSpec((1, tm, KVW), lambda bi, j: (bi, j, 0))
    bs_vt = pl.BlockSpec((1, KV_HEADS, V_ROWS, tm), lambda bi, j: (bi, 0, 0, j))
    sd = jax.ShapeDtypeStruct
    return pl.pallas_call(
        _mixin_kernel,
        grid=(b, l // tm),
        in_specs=[
            pl.BlockSpec((1, tm, d), lambda bi, j: (bi, j, 0)),
            pl.BlockSpec((1, 6, d), lambda bi, j: (mod_row(bi), 0, 0)),
            pl.BlockSpec((1, d), lambda bi, j: (0, 0)),
            pl.BlockSpec(wt.shape, lambda bi, j: (0, 0)),
            pl.BlockSpec(wn.shape, lambda bi, j: (0, 0)),
            pl.BlockSpec((HEAD_DIM, 1), lambda bi, j: (0, 0)),
            pl.BlockSpec((HEAD_DIM, 1), lambda bi, j: (0, 0)),
            pl.BlockSpec((HALF, tm), lambda bi, j: (0, j)),
            pl.BlockSpec((HALF, tm), lambda bi, j: (0, j)),
        ],
        out_specs=[bs_qt, bs_k, bs_vt, pl.BlockSpec((1, tm, FW), lambda bi, j: (bi, j, 0)), bs_qt, bs_k, bs_vt],
        out_shape=[sd((b, QW, l), BF16), sd((b, l, KVW), BF16), sd((b, KV_HEADS, V_ROWS, l), BF16),
                   sd((b, l, FW), BF16),
                   sd((b, QW, l), BF16), sd((b, l, KVW), BF16), sd((b, KV_HEADS, V_ROWS, l), BF16)],
        compiler_params=_cparams(("parallel", "arbitrary")),
        name="mixer_inputs",
    )(x, mods, g, wt, wn, qn, kn, cos_t, sin_t)


def _attn_kernel(*refs, bq, bk, n_main, unroll, window, has_sink, seq):
    refs = list(refs)
    qt_ref, kctx_ref, vctx_ref = refs[:3]
    pos = 3
    if n_main:
        qnext_ref = refs[pos]
        pos += 1
    if n_main or window:
        k_ref, vt_ref = refs[pos:pos + 2]
        pos += 2
    if has_sink:
        sink_ref = refs[pos]
        pos += 1
    o_ref, ot_ref, qp_ref, f_scr, fm_scr = refs[pos:pos + 5]
    if n_main:
        acc_ref, m_ref, s_scr, cm_scr = refs[pos + 5:pos + 9]
    n = GROUP * bq
    n_ctx = kctx_ref.shape[1]
    zeros_pad = jnp.zeros((HEAD_DIM, n), BF16)

    if window:
        band = bq + 2 * WINDOW
        q0 = pl.program_id(1) * bq
        start = pl.multiple_of(jnp.clip(q0 - WINDOW, 0, seq - band), WINDOW)
        kpos = start + lax.broadcasted_iota(jnp.int32, (band, bq), 0)
        qpos = q0 + lax.broadcasted_iota(jnp.int32, (band, bq), 1)
        bias1 = jnp.where(jnp.abs(kpos - qpos) <= WINDOW, 0.0, NEG_INF).astype(F32)
        bias = jnp.concatenate([bias1] * GROUP, axis=1)

    cur = pl.program_id(1) % 2 if n_main else 0
    nxt = 1 - cur

    def logits(which, g, keys):
        return jnp.dot(keys, qp_ref[which, g], preferred_element_type=F32)

    def pad_queries(q_ref, which, g):
        qg = jnp.concatenate([q_ref[0, (GROUP * g + h) * HEAD_DIM:(GROUP * g + h + 1) * HEAD_DIM, :]
                              for h in range(GROUP)], axis=1)
        qp_ref[which, g] = jnp.concatenate([qg, zeros_pad] if g == 0 else [zeros_pad, qg], axis=0)

    def first_logits(which, g):
        s = logits(which, g, kctx_ref[0])
        f_scr[g, :n_ctx] = s
        m = jnp.max(s, axis=0, keepdims=True)
        if window:
            s = logits(which, g, k_ref[0, pl.ds(start, band), :]) + bias
            f_scr[g, n_ctx:] = s
            m = jnp.maximum(m, jnp.max(s, axis=0, keepdims=True))
        fm_scr[g] = m

    def qk_to(which, g, slot, j):
        off = pl.multiple_of(j * bk, bk)
        s = logits(which, g, k_ref[0, pl.ds(off, bk), :])
        s_scr[g, slot] = s
        cm_scr[g, slot] = jnp.max(s, axis=0, keepdims=True)

    def softmax_pv(g, slot, j):
        off = pl.multiple_of(j * bk, bk)
        m_old = m_ref[g]
        m_new = jnp.maximum(m_old, cm_scr[g, slot])
        p = jnp.exp2(s_scr[g, slot] - m_new).astype(BF16)
        acc_ref[g] = acc_ref[g] * jnp.exp2(m_old - m_new) + jnp.dot(
            vt_ref[0, g, :, pl.ds(off, bk)], p, preferred_element_type=F32)
        m_ref[g] = m_new

    def stage_first():
        for g in range(KV_HEADS):
            pad_queries(qt_ref, cur, g)
            first_logits(cur, g)
            if n_main:
                for a in range(AHEAD):
                    qk_to(cur, g, a, a)

    if n_main:
        pl.when(pl.program_id(1) == 0)(stage_first)
    else:
        stage_first()

    first = []
    for g in range(KV_HEADS):
        vt = vctx_ref[0, g]
        if window:
            vt = jnp.concatenate([vt, vt_ref[0, g, :, pl.ds(start, band)]], axis=1)
        m = fm_scr[g]
        if has_sink:
            sink_row = jnp.concatenate(
                [jnp.full((1, bq), sink_ref[GROUP * g + h] * LOG2E, F32) for h in range(GROUP)], axis=1)
            m = jnp.maximum(m, sink_row)
        acc = jnp.dot(vt, jnp.exp2(f_scr[g] - m).astype(BF16), preferred_element_type=F32)
        den = acc[HEAD_DIM:HEAD_DIM + 1]
        if has_sink:
            den = den + jnp.exp2(sink_row - m)
        if n_main:
            m_ref[g] = m
            acc_ref[g] = acc
        first.append((acc, den))

    if n_main:
        for g in range(KV_HEADS):
            pad_queries(qnext_ref, nxt, g)
        trips = n_main // unroll

        def body(i, carry):
            for u in range(unroll):
                j = unroll * i + u
                if u + AHEAD < unroll:
                    which, jn = cur, j + AHEAD
                else:
                    wrap = i == trips - 1
                    which = jnp.where(wrap, nxt, cur)
                    jn = jnp.where(wrap, u + AHEAD - unroll, j + AHEAD)
                for g in range(KV_HEADS):
                    qk_to(which, g, (u + AHEAD) % unroll, jn)
                    softmax_pv(g, u, j)
            return carry
        lax.fori_loop(0, trips, body, 0)
        for g in range(KV_HEADS):
            first_logits(nxt, g)

    for g in range(KV_HEADS):
        if n_main:
            acc = acc_ref[g]
            den = acc[HEAD_DIM:HEAD_DIM + 1]
        else:
            acc, den = first[g]
        o = acc[:HEAD_DIM] / den
        for h in range(GROUP):
            ot_ref[(GROUP * g + h) * HEAD_DIM:(GROUP * g + h + 1) * HEAD_DIM, :] = o[:, h * bq:(h + 1) * bq]

    o_ref[0] = ot_ref[...].T.astype(BF16)


def _attention(qt, k_ctx, vt_ctx, k=None, vt=None, sink=None, *, window=False, bq=ATTN_BQ, bk=ATTN_BK,
               unroll=ATTN_TRIP, name="attn"):
    b, _, lq = qt.shape
    c = k_ctx.shape[1]
    bq = min(bq, lq)
    nq = lq // bq
    n_main = 0
    seq = 0
    args = [qt, k_ctx, vt_ctx]
    in_specs = [
        pl.BlockSpec((1, QW, bq), lambda bi, i: (bi, 0, i)),
        pl.BlockSpec((1, c, KVW), lambda bi, i: (bi, 0, 0)),
        pl.BlockSpec((1, KV_HEADS, V_ROWS, c), lambda bi, i: (bi, 0, 0, 0)),
    ]
    if k is not None:
        seq = k.shape[1]
        bk = min(bk, seq)
        n_main = 0 if window else seq // bk
        if n_main:
            args.append(qt)
            in_specs.append(pl.BlockSpec((1, QW, bq), lambda bi, i: (bi, 0, jnp.minimum(i + 1, nq - 1))))
        args += [k, vt]
        in_specs += [
            pl.BlockSpec((1, seq, KVW), lambda bi, i: (bi, 0, 0)),
            pl.BlockSpec((1, KV_HEADS, V_ROWS, seq), lambda bi, i: (bi, 0, 0, 0)),
        ]
    if sink is not None:
        args.append(sink)
        in_specs.append(pl.BlockSpec(memory_space=pltpu.SMEM))
    n = GROUP * bq
    first_keys = c + (bq + 2 * WINDOW if window else 0)
    scratch = [pltpu.VMEM((QW, bq), F32), pltpu.VMEM((2, KV_HEADS, 2 * HEAD_DIM, n), BF16),
               pltpu.VMEM((KV_HEADS, first_keys, n), F32), pltpu.VMEM((KV_HEADS, 1, n), F32)]
    if n_main:
        unroll = min(unroll, n_main)
        assert unroll > AHEAD and n_main % unroll == 0 and sink is None
        scratch += [pltpu.VMEM((KV_HEADS, V_ROWS, n), F32), pltpu.VMEM((KV_HEADS, 1, n), F32),
                    pltpu.VMEM((KV_HEADS, unroll, bk, n), F32), pltpu.VMEM((KV_HEADS, unroll, 1, n), F32)]
    kern = functools.partial(_attn_kernel, bq=bq, bk=bk, n_main=n_main, unroll=unroll, window=window,
                             has_sink=sink is not None, seq=seq)
    return pl.pallas_call(
        kern,
        grid=(b, nq),
        in_specs=in_specs,
        out_specs=pl.BlockSpec((1, bq, QW), lambda bi, i: (bi, i, 0)),
        out_shape=jax.ShapeDtypeStruct((b, lq, QW), BF16),
        scratch_shapes=scratch,
        compiler_params=_cparams(("parallel", "arbitrary")),
        name=name,
    )(*args)


def _dft_cs(n):
    idx = np.arange(n)
    ang = 2.0 * np.pi * ((idx[:, None] * idx[None, :]) % n) / n
    return np.cos(ang), np.sin(ang)


def _mxu_table(table):
    return jnp.asarray(table, F32).astype(BF16)


def _channel_table():
    c, s = _dft_cs(F_CH)
    eye = np.eye(F_GROUPS)
    return np.concatenate([np.kron(eye, c), np.kron(eye, s)], axis=0)


def _fft_stage1_kernel(w_ref, x_ref, y_ref):
    y_ref[0] = jnp.dot(w_ref[...], x_ref[0], preferred_element_type=F32).astype(BF16)


def _fft_stage2_kernel(t_ref, cs_ref, y_ref, o_ref, *, kc, scale):
    for kk in range(kc):
        a = jnp.concatenate([y_ref[0, 0, kk], y_ref[0, 1, kk]], axis=0)
        z = jnp.dot(t_ref[kk], a, preferred_element_type=F32)
        zc = jnp.concatenate([z[:FFT_L2], z[FFT_L2:]], axis=1).astype(BF16)
        o = jnp.dot(zc, cs_ref[...], preferred_element_type=F32) * scale
        o_ref[0, :, kk * FW:(kk + 1) * FW] = o.astype(BF16)


def _fourier_mix(u):
    b, l, _ = u.shape
    l1 = l // FFT_L2
    c1, s1 = _dft_cs(l1)
    w1 = _mxu_table(np.concatenate([c1, -s1], axis=0))
    k = (np.arange(l1)[:, None, None] + l1 * np.arange(FFT_L2)[None, :, None])
    ang = 2.0 * np.pi * ((k * np.arange(FFT_L2)[None, None, :]) % l) / l
    mc, ms = np.cos(ang), np.sin(ang)
    t2 = _mxu_table(np.concatenate([np.concatenate([mc, ms], axis=2),
                                    np.concatenate([-ms, mc], axis=2)], axis=1))
    cs = _mxu_table(_channel_table())

    cols = FFT_L2 * FW
    tc = min(FFT1_COLS, cols)
    y = pl.pallas_call(
        _fft_stage1_kernel,
        grid=(b, cols // tc),
        in_specs=[pl.BlockSpec((2 * l1, l1), lambda bi, j: (0, 0)),
                  pl.BlockSpec((1, l1, tc), lambda bi, j: (bi, 0, j))],
        out_specs=pl.BlockSpec((1, 2 * l1, tc), lambda bi, j: (bi, 0, j)),
        out_shape=jax.ShapeDtypeStruct((b, 2 * l1, cols), BF16),
        compiler_params=_cparams(("parallel", "arbitrary")),
        name="fft_stage1",
    )(w1, u.reshape(b, l1, cols))

    kc = min(FFT2_K1, l1)
    scale = 1.0 / math.sqrt(l * F_CH)
    out = pl.pallas_call(
        functools.partial(_fft_stage2_kernel, kc=kc, scale=scale),
        grid=(l1 // kc, b),
        in_specs=[pl.BlockSpec((kc, 2 * FFT_L2, 2 * FFT_L2), lambda j, bi: (j, 0, 0)),
                  pl.BlockSpec((2 * FW, FW), lambda j, bi: (0, 0)),
                  pl.BlockSpec((1, 2, kc, FFT_L2, FW), lambda j, bi: (bi, 0, j, 0, 0))],
        out_specs=pl.BlockSpec((1, FFT_L2, kc * FW), lambda j, bi: (bi, 0, j)),
        out_shape=jax.ShapeDtypeStruct((b, FFT_L2, l1 * FW), BF16),
        compiler_params=_cparams(("arbitrary", "arbitrary")),
        name="fft_stage2",
    )(t2, cs, y.reshape(b, 2, l1, FFT_L2, FW))
    return out.reshape(b, l, FW)


def _fourier_small_kernel(wl_ref, cst_ref, u_ref, o_ref, *, scale):
    p = jnp.dot(u_ref[0], cst_ref[...], preferred_element_type=F32)
    pp = jnp.concatenate([p[:, :FW], p[:, FW:]], axis=0).astype(BF16)
    o_ref[0] = (jnp.dot(wl_ref[...], pp, preferred_element_type=F32) * scale).astype(BF16)


def _fourier_mix_small(u):
    b, l, _ = u.shape
    cl, sl = _dft_cs(l)
    wl = _mxu_table(np.concatenate([cl, -sl], axis=1))
    cst = _mxu_table(_channel_table().reshape(2, FW, FW).transpose(1, 0, 2).reshape(FW, 2 * FW))
    return pl.pallas_call(
        functools.partial(_fourier_small_kernel, scale=1.0 / math.sqrt(l * F_CH)),
        grid=(b,),
        in_specs=[pl.BlockSpec((l, 2 * l), lambda bi: (0, 0)),
                  pl.BlockSpec((FW, 2 * FW), lambda bi: (0, 0)),
                  pl.BlockSpec((1, l, FW), lambda bi: (bi, 0, 0))],
        out_specs=pl.BlockSpec((1, l, FW), lambda bi: (bi, 0, 0)),
        out_shape=jax.ShapeDtypeStruct((b, l, FW), BF16),
        compiler_params=_cparams(("parallel",)),
        name="fourier_ctx",
    )(wl, cst, u)


def _ffn_kernel(x_ref, og_ref, of_ref, ow_ref, mods_ref, g_ref, wo_ref, wg_ref, wu_ref, wd_ref, gf_ref, out_ref,
                *, final_norm):
    gt1, sh2 = mods_ref[0, 2:3, :], mods_ref[0, 3:4, :]
    sc2, gt2 = mods_ref[0, 4:5, :], mods_ref[0, 5:6, :]
    o = jnp.concatenate([og_ref[0], of_ref[0], ow_ref[0]], axis=1)
    x1 = x_ref[0] + gt1 * jnp.dot(o, wo_ref[...], preferred_element_type=F32)
    hn = (_rms(x1, g_ref[...]) * (1.0 + sc2) + sh2).astype(BF16)
    gate = jnp.dot(hn, wg_ref[...], preferred_element_type=F32)
    up = jnp.dot(hn, wu_ref[...], preferred_element_type=F32)
    act = (gate / (1.0 + jnp.exp(-gate)) * up).astype(BF16)
    x2 = x1 + gt2 * jnp.dot(act, wd_ref[...], preferred_element_type=F32)
    if final_norm:
        x2 = _rms(x2, gf_ref[...])
    out_ref[0] = x2


def _out_ffn(x, og, of, ow, mods, mod_row, g_ffn, wo, wg, wu, wd, g_final, *, final_norm):
    b, l, d = x.shape
    tm = min(FFN_ROWS, l)
    resident = functools.partial(pl.BlockSpec, pipeline_mode=pl.Buffered(1))
    row = lambda w: pl.BlockSpec((1, tm, w), lambda bi, j: (bi, j, 0))
    return pl.pallas_call(
        functools.partial(_ffn_kernel, final_norm=final_norm),
        grid=(b, l // tm),
        in_specs=[
            row(d), row(QW), row(FW), row(QW),
            pl.BlockSpec((1, 6, d), lambda bi, j: (mod_row(bi), 0, 0)),
            pl.BlockSpec((1, d), lambda bi, j: (0, 0)),
            resident(wo.shape, lambda bi, j: (0, 0)),
            resident(wg.shape, lambda bi, j: (0, 0)),
            resident(wu.shape, lambda bi, j: (0, 0)),
            resident(wd.shape, lambda bi, j: (0, 0)),
            pl.BlockSpec((1, d), lambda bi, j: (0, 0)),
        ],
        out_specs=row(d),
        out_shape=jax.ShapeDtypeStruct((b, l, d), F32),
        compiler_params=_cparams(("parallel", "arbitrary")),
        name="out_ffn",
    )(x, og, of, ow, mods, g_ffn, wo, wg, wu, wd, g_final)


def _rope_tables_t(n_tokens):
    pos = np.arange(n_tokens)
    inv_freq = ROPE_THETA ** (-np.arange(0, HALF, 2, dtype=np.float64) / HALF)
    ang = np.concatenate([(pos // GRID_W)[None, :] * inv_freq[:, None],
                          (pos % GRID_W)[None, :] * inv_freq[:, None]], axis=0)
    return jnp.asarray(np.cos(ang), F32), jnp.asarray(np.sin(ang), F32)


def kernel(x, c, ctx, c_ctx, w_ada, b_ada, g_mix, g_ffn, w_in, q_norm, k_norm, sink, w_out, w_gate, w_up, w_down,
           g_final):
    batch, seq, d = x.shape
    n_ctx = ctx.shape[1]
    depth = w_ada.shape[0]
    ctx_row = batch
    rows = -(-(batch + 1) // 8) * 8

    cond = jnp.zeros((rows, d), F32).at[:batch].set(c).at[ctx_row].set(c_ctx)
    mods = _adaln(cond, w_ada, b_ada).reshape(depth * rows, 6, d)

    cos_t, sin_t = _rope_tables_t(seq)
    cos_1, sin_0 = jnp.ones((HALF, n_ctx), F32), jnp.zeros((HALF, n_ctx), F32)

    a_w = QW + 2 * KVW
    w_in_b = w_in.astype(BF16)
    w_t = jnp.concatenate([w_in_b[:, :, :a_w], w_in_b[:, :, a_w + FW:]], axis=2).transpose(0, 2, 1)
    w_n = w_in_b[:, :, a_w:a_w + FW]
    w_out_b, w_gate_b, w_up_b, w_down_b = (w.astype(BF16) for w in (w_out, w_gate, w_up, w_down))
    g_fin = g_final.reshape(1, d)

    for l in range(depth):
        x_row = lambda bi, l=l: l * rows + bi
        c_row = lambda bi, l=l: l * rows + ctx_row
        g_m, g_f = g_mix[l].reshape(1, d), g_ffn[l].reshape(1, d)
        qn, kn = q_norm[l].reshape(HEAD_DIM, 1), k_norm[l].reshape(HEAD_DIM, 1)
        last = l == depth - 1

        qa, ka, va, fb, qc, kc, vc = _mixer_inputs(x, mods, x_row, g_m, w_t[l], w_n[l], qn, kn, cos_t, sin_t)
        qac, kac, vac, fbc, qcc, kcc, vcc = _mixer_inputs(ctx, mods, c_row, g_m, w_t[l], w_n[l], qn, kn,
                                                          cos_1, sin_0)
        og = _attention(qa, kac, vac, ka, va, name="attn_global")
        of = _fourier_mix(fb)
        ow = _attention(qc, kcc, vcc, kc, vc, sink[l], window=True, name="attn_window")
        x = _out_ffn(x, og, of, ow, mods, x_row, g_f, w_out_b[l], w_gate_b[l], w_up_b[l], w_down_b[l], g_fin,
                     final_norm=last)
        if not last:
            ogc = _attention(qac, kac, vac, name="attn_ctx_global")
            ofc = _fourier_mix_small(fbc)
            owc = _attention(qcc, kcc, vcc, sink=sink[l], name="attn_ctx_sink")
            ctx = _out_ffn(ctx, ogc, ofc, owc, mods, c_row, g_f, w_out_b[l], w_gate_b[l], w_up_b[l], w_down_b[l],
                           g_fin, final_norm=False)
    return x
```
